```python
import math
import jax, jax.numpy as jnp
from jax import lax
import numpy as np

D_MODEL = 2048
BATCH = 4
SEQ = 8192
DEPTH = 4

N_MIXERS = 2
N_MEM = 256
MIX_WIDTH = 2 * D_MODEL
MEM_WIDTH = MIX_WIDTH // 4
TOK_WIDTH = MIX_WIDTH - MEM_WIDTH
MEM_HEADS = 4
MEM_HEAD_DIM = MEM_WIDTH // MEM_HEADS

SSD_HEAD_DIM = 64
SSD_HEADS = TOK_WIDTH // SSD_HEAD_DIM
SSD_GROUPS = 8
SSD_HEADS_PER_GROUP = SSD_HEADS // SSD_GROUPS
SSD_STATE = 128
SSD_CONV = 4
SSD_CHUNK = 128
SSD_CONV_DIM = TOK_WIDTH + 2 * SSD_GROUPS * SSD_STATE
SSD_IN_COLS = SSD_CONV_DIM + SSD_HEADS + MEM_WIDTH + MIX_WIDTH

ATTN_HEAD_DIM = 128
ATTN_HEADS_PER_GROUP = TOK_WIDTH // ATTN_HEAD_DIM
DILATED_GROUPS = ((128, 1), (512, 4), (2048, 16))
N_DIL = len(DILATED_GROUPS)
N_ALIBI_HEADS = N_DIL * ATTN_HEADS_PER_GROUP
ALIBI_MAX_EXP = 8.0
ATTN_GROUP_COLS = 3 * TOK_WIDTH
ATTN_IN_COLS = N_DIL * ATTN_GROUP_COLS + MEM_WIDTH + MIX_WIDTH
ATTN_BLOCK = 128
EPS = 1e-6

kernel_name = 'hybrid_ssd_dilated_memory_trunk'


def _rmsnorm(x, g):
    xf = x.astype(jnp.float32)
    xf = xf * lax.rsqrt(jnp.mean(xf * xf, axis=-1, keepdims=True) + EPS)
    return xf.astype(x.dtype) * g


def _grouped_rmsnorm(x, g, groups):
    shp = x.shape
    xg = x.reshape(shp[:-1] + (groups, shp[-1] // groups)).astype(jnp.float32)
    xg = xg * lax.rsqrt(jnp.mean(xg * xg, axis=-1, keepdims=True) + EPS)
    return xg.reshape(shp).astype(x.dtype) * g


def _causal_depthwise_conv(u, w, b):
    y = lax.conv_general_dilated(u, w[:, None, :], window_strides=(1,), padding=[(SSD_CONV - 1, 0)],
                                 dimension_numbers=('NWC', 'WIO', 'NWC'), feature_group_count=u.shape[-1])
    return y + b


def _memory_cross_attention(q_mem, mem_n, w_mem_kv):
    b_, t_, _ = q_mem.shape
    mk, mv = jnp.split(mem_n @ w_mem_kv, 2, axis=-1)
    q = q_mem.reshape(b_, t_, MEM_HEADS, MEM_HEAD_DIM)
    mk = mk.reshape(b_, -1, MEM_HEADS, MEM_HEAD_DIM)
    mv = mv.reshape(b_, -1, MEM_HEADS, MEM_HEAD_DIM)
    s = jnp.einsum('bthe,bmhe->bhtm', q, mk).astype(jnp.float32) * (MEM_HEAD_DIM ** -0.5)
    p = jax.nn.softmax(s, axis=-1).astype(mv.dtype)
    return jnp.einsum('bhtm,bmhe->bthe', p, mv).reshape(b_, t_, MEM_WIDTH)


def _ssd_chunked(xh, dt, a, bm, cm):
    b_, t_ = xh.shape[:2]
    nc = t_ // SSD_CHUNK
    def chunk(v):
        return v.reshape((b_, nc, SSD_CHUNK) + v.shape[2:])
    xc, dtc, bc, cc = chunk(xh), chunk(dt), chunk(bm), chunk(cm)
    a_cs = jnp.cumsum(dtc * a, axis=2)
    xdt = xc * dtc[..., None]
    pos = jnp.arange(SSD_CHUNK)
    causal = (pos[:, None] >= pos[None, :])[None, None, :, :, None, None]
    seg = a_cs[:, :, :, None] - a_cs[:, :, None, :]
    decay_ls = jnp.exp(jnp.where(causal, seg, -jnp.inf))
    cb = jnp.einsum('bclgn,bcsgn->bclsg', cc, bc)
    y_diag = jnp.einsum('bclsg,bclsgh,bcsghp->bclghp', cb, decay_ls, xdt)
    decay_to_end = jnp.exp(a_cs[:, :, -1:] - a_cs)
    chunk_states = jnp.einsum('bcsgn,bcsgh,bcsghp->bcghpn', bc, decay_to_end, xdt)
    chunk_decay = jnp.exp(a_cs[:, :, -1])

    def step(h, inp):
        st, dec = inp
        return dec[..., None, None] * h + st, h

    h0 = jnp.zeros((b_,) + chunk_states.shape[2:], chunk_states.dtype)
    _, h_prev = lax.scan(step, h0, (jnp.moveaxis(chunk_states, 1, 0), jnp.moveaxis(chunk_decay, 1, 0)))
    h_prev = jnp.moveaxis(h_prev, 0, 1)
    y_off = jnp.einsum('bclgn,bcghpn,bclgh->bclghp', cc, h_prev, jnp.exp(a_cs))
    return (y_diag + y_off).reshape(xh.shape)


def _dilated_group_attention(q, k, v, window, dilation, slopes):
    b_, t_, nh, e_ = q.shape
    n_sub = t_ // dilation
    w = window // dilation
    c = min(ATTN_BLOCK, n_sub)
    nb = -(-n_sub // c)
    lp = nb * c
    tail = lp - n_sub
    def strided(arr):
        return arr.reshape(b_, n_sub, dilation, nh, e_)
    qs = jnp.pad(strided(q), ((0, 0), (0, tail), (0, 0), (0, 0), (0, 0)))
    ks = jnp.pad(strided(k), ((0, 0), (w, tail), (0, 0), (0, 0), (0, 0)))
    vs = jnp.pad(strided(v), ((0, 0), (w, tail), (0, 0), (0, 0), (0, 0)))
    rel = jnp.arange(c)[:, None] + w - jnp.arange(c + w)[None, :]
    band = (rel >= 0) & (rel <= w)
    bias = -slopes[:, None, None] * (dilation * rel).astype(jnp.float32)[None]
    scale = e_ ** -0.5

    def block(n):
        start = n * c
        qb = lax.dynamic_slice_in_dim(qs, start, c, axis=1)
        kb = lax.dynamic_slice_in_dim(ks, start, c + w, axis=1)
        vb = lax.dynamic_slice_in_dim(vs, start, c + w, axis=1)
        key_pos = start - w + jnp.arange(c + w)
        valid = band & (key_pos >= 0)[None, :]
        s = jnp.einsum('bqrhe,bkrhe->brhqk', qb, kb).astype(jnp.float32) * scale + bias
        s = jnp.where(valid, s, -jnp.inf)
        m = jnp.max(s, axis=-1, keepdims=True)
        p = jnp.exp(s - m)
        den = jnp.transpose(jnp.sum(p, axis=-1), (0, 3, 1, 2))
        o = jnp.einsum('brhqk,bkrhe->bqrhe', p.astype(vb.dtype), vb).astype(jnp.float32)
        lse = jnp.transpose(m[..., 0], (0, 3, 1, 2)) + jnp.log(den)
        return o / den[..., None], lse

    o, lse = lax.map(block, jnp.arange(nb))
    o = jnp.moveaxis(o, 0, 1).reshape(b_, lp, dilation, nh, e_)[:, :n_sub].reshape(b_, t_, nh, e_)
    lse = jnp.moveaxis(lse, 0, 1).reshape(b_, lp, dilation, nh)[:, :n_sub].reshape(b_, t_, nh)
    return o, lse


def _ssd_layer(x, mem_n, norm_g, w_in, conv_w, conv_b, dt_bias, a_log, d_skip, ssd_norm_g, w_mem_kv, w_out):
    b_, t_, _ = x.shape
    h = _rmsnorm(x, norm_g)
    proj = h @ w_in
    xbc, dt_raw, q_mem, z = jnp.split(
        proj, [SSD_CONV_DIM, SSD_CONV_DIM + SSD_HEADS, SSD_CONV_DIM + SSD_HEADS + MEM_WIDTH], axis=-1)
    xbc = jax.nn.silu(_causal_depthwise_conv(xbc, conv_w, conv_b))
    xs, bm, cm = jnp.split(xbc, [TOK_WIDTH, TOK_WIDTH + SSD_GROUPS * SSD_STATE], axis=-1)
    xs = xs.reshape(b_, t_, SSD_GROUPS, SSD_HEADS_PER_GROUP, SSD_HEAD_DIM)
    bm = bm.reshape(b_, t_, SSD_GROUPS, SSD_STATE)
    cm = cm.reshape(b_, t_, SSD_GROUPS, SSD_STATE)
    dt = jax.nn.softplus(dt_raw.astype(jnp.float32) + dt_bias.astype(jnp.float32))
    dt = dt.reshape(b_, t_, SSD_GROUPS, SSD_HEADS_PER_GROUP)
    a = -jnp.exp(a_log.astype(jnp.float32)).reshape(SSD_GROUPS, SSD_HEADS_PER_GROUP)
    y = _ssd_chunked(xs, dt, a, bm, cm) + d_skip.reshape(SSD_GROUPS, SSD_HEADS_PER_GROUP, 1) * xs
    y_tok = y.reshape(b_, t_, TOK_WIDTH).astype(x.dtype)
    y_mem = _memory_cross_attention(q_mem, mem_n, w_mem_kv)
    gated = jnp.concatenate([y_tok, y_mem], axis=-1) * jax.nn.silu(z)
    gated = jnp.concatenate([_grouped_rmsnorm(gated[..., :TOK_WIDTH], ssd_norm_g, SSD_GROUPS),
                             gated[..., TOK_WIDTH:]], axis=-1)
    return x + gated @ w_out


def _dilated_attention_layer(x, mem_n, norm_g, w_in, w_mem_kv, w_out):
    b_, t_, _ = x.shape
    h = _rmsnorm(x, norm_g)
    slopes = jnp.exp2(-ALIBI_MAX_EXP * jnp.arange(1, N_ALIBI_HEADS + 1, dtype=jnp.float32) / N_ALIBI_HEADS)
    slopes = slopes.reshape(N_DIL, ATTN_HEADS_PER_GROUP)
    outs, lses = [], []
    for g, (window, dilation) in enumerate(DILATED_GROUPS):
        qkv = (h @ w_in[:, g * ATTN_GROUP_COLS:(g + 1) * ATTN_GROUP_COLS])
        qkv = qkv.reshape(b_, t_, 3, ATTN_HEADS_PER_GROUP, ATTN_HEAD_DIM)
        o, lse = _dilated_group_attention(qkv[:, :, 0], qkv[:, :, 1], qkv[:, :, 2], window, dilation, slopes[g])
        outs.append(o)
        lses.append(lse)
    wts = jax.nn.softmax(jnp.stack(lses), axis=0)
    y_tok = jnp.einsum('gbth,gbthe->bthe', wts, jnp.stack(outs)).reshape(b_, t_, TOK_WIDTH).astype(x.dtype)
    q_mem, z = jnp.split(h @ w_in[:, N_DIL * ATTN_GROUP_COLS:], [MEM_WIDTH], axis=-1)
    y_mem = _memory_cross_attention(q_mem, mem_n, w_mem_kv)
    gated = jnp.concatenate([y_tok, y_mem], axis=-1) * jax.nn.silu(z)
    return x + gated @ w_out


def setup_inputs(seed: int = 0) -> dict:
    key = jax.random.key(seed)
    keys = iter(jax.random.split(key, 64))
    f32 = jnp.float32

    def nrm(shape, scale):
        return scale * jax.random.normal(next(keys), shape, f32)

    def gain(n):
        return 1.0 + nrm((n,), 0.02)

    inp = {
        'x': nrm((BATCH, SEQ, D_MODEL), 1.0),
        'mem': nrm((BATCH, N_MEM, D_MODEL), 1.0),
        'mem_norm_g': gain(D_MODEL),
        'final_norm_g': gain(D_MODEL),
    }
    for i in range(DEPTH):
        inp[f'norm_g_{i}'] = gain(D_MODEL)
        if i % N_MIXERS == 0:
            inp[f'w_in_{i}'] = nrm((D_MODEL, SSD_IN_COLS), D_MODEL ** -0.5)
            inp[f'conv_w_{i}'] = nrm((SSD_CONV, SSD_CONV_DIM), SSD_CONV ** -0.5)
            inp[f'conv_b_{i}'] = nrm((SSD_CONV_DIM,), 0.01)
            dt0 = jnp.exp(jax.random.uniform(next(keys), (SSD_HEADS,), f32, math.log(1e-3), math.log(1e-1)))
            inp[f'dt_bias_{i}'] = dt0 + jnp.log(-jnp.expm1(-dt0))
            inp[f'a_log_{i}'] = jnp.log(jax.random.uniform(next(keys), (SSD_HEADS,), f32, 1.0, 16.0))
            inp[f'd_skip_{i}'] = gain(SSD_HEADS)
            inp[f'ssd_norm_g_{i}'] = gain(TOK_WIDTH)
        else:
            inp[f'w_in_{i}'] = nrm((D_MODEL, ATTN_IN_COLS), D_MODEL ** -0.5)
        inp[f'w_mem_kv_{i}'] = nrm((D_MODEL, 2 * MEM_WIDTH), D_MODEL ** -0.5)
        inp[f'w_out_{i}'] = nrm((MIX_WIDTH, D_MODEL), MIX_WIDTH ** -0.5)
    return inp


def reference(x, mem, mem_norm_g, final_norm_g,
              norm_g_0, w_in_0, conv_w_0, conv_b_0, dt_bias_0, a_log_0, d_skip_0, ssd_norm_g_0, w_mem_kv_0, w_out_0,
              norm_g_1, w_in_1, w_mem_kv_1, w_out_1,
              norm_g_2, w_in_2, conv_w_2, conv_b_2, dt_bias_2, a_log_2, d_skip_2, ssd_norm_g_2, w_mem_kv_2, w_out_2,
              norm_g_3, w_in_3, w_mem_kv_3, w_out_3):
    mem_n = _rmsnorm(mem, mem_norm_g)
    params = [
        (norm_g_0, w_in_0, conv_w_0, conv_b_0, dt_bias_0, a_log_0, d_skip_0, ssd_norm_g_0, w_mem_kv_0, w_out_0),
        (norm_g_1, w_in_1, w_mem_kv_1, w_out_1),
        (norm_g_2, w_in_2, conv_w_2, conv_b_2, dt_bias_2, a_log_2, d_skip_2, ssd_norm_g_2, w_mem_kv_2, w_out_2),
        (norm_g_3, w_in_3, w_mem_kv_3, w_out_3),
    ]
    for i in range(DEPTH):
        layer_fn = _ssd_layer if i % N_MIXERS == 0 else _dilated_attention_layer
        x = layer_fn(x, mem_n, *params[i])
    return _rmsnorm(x, final_norm_g)
```

```python
import functools

import jax
import jax.numpy as jnp
from jax import lax
from jax.experimental import pallas as pl
from jax.experimental.pallas import tpu as pltpu

F32 = jnp.float32
BF16 = jnp.bfloat16
EPS = 1e-6
NEG = -1e30

LANES = 128
D_MODEL = 2048
N_MEM = 256
MIX_WIDTH = 2 * D_MODEL
MEM_WIDTH = MIX_WIDTH // 4
TOK_WIDTH = MIX_WIDTH - MEM_WIDTH
MEM_HEADS = 4
MEM_HEAD_DIM = MEM_WIDTH // MEM_HEADS

SSD_HEAD_DIM = 64
SSD_HEADS = TOK_WIDTH // SSD_HEAD_DIM
SSD_GROUPS = 8
SSD_HEADS_PER_GROUP = SSD_HEADS // SSD_GROUPS
SSD_GROUP_COLS = SSD_HEADS_PER_GROUP * SSD_HEAD_DIM
SSD_PAIRS_PER_GROUP = SSD_GROUP_COLS // LANES
SSD_STATE = 128
SSD_CONV = 4
SSD_CHUNK = 128
SSD_BC_COLS = SSD_GROUPS * SSD_STATE
SSD_CONV_DIM = TOK_WIDTH + 2 * SSD_BC_COLS
HALO_ROWS = 8

ATTN_HEAD_DIM = 128
ATTN_HEADS = TOK_WIDTH // ATTN_HEAD_DIM
DILATED_GROUPS = ((128, 1), (512, 4), (2048, 16))
N_DIL = len(DILATED_GROUPS)
ATTN_GROUP_COLS = 3 * TOK_WIDTH
ATTN_BLOCK = 128
ALIBI_MAX_EXP = 8.0

VMEM_LIMIT = 56 * 1024 * 1024


def _cparams(*sem):
    return pltpu.CompilerParams(dimension_semantics=sem, vmem_limit_bytes=VMEM_LIMIT)


def _sigmoid(v):
    return 1.0 / (1.0 + jnp.exp(-v))


def _split3(v):
    hi = v.astype(BF16)
    r1 = v - hi.astype(F32)
    mid = r1.astype(BF16)
    lo = (r1 - mid.astype(F32)).astype(BF16)
    return hi, mid, lo


def _rms_matmul_kernel(x_ref, g_ref, w_ref, o_ref, h_ref, *, row_chunk):
    @pl.when(pl.program_id(1) == 0)
    def _():
        def body(i, carry):
            r = pl.multiple_of(i * row_chunk, row_chunk)
            xf = x_ref[pl.ds(r, row_chunk), :]
            ms = jnp.mean(xf * xf, axis=-1, keepdims=True)
            h_ref[pl.ds(r, row_chunk), :] = ((xf * lax.rsqrt(ms + EPS)) * g_ref[...]).astype(BF16)
            return carry
        lax.fori_loop(0, x_ref.shape[0] // row_chunk, body, 0)

    o_ref[...] = jnp.dot(h_ref[...], w_ref[...], preferred_element_type=F32).astype(o_ref.dtype)


def _rms_matmul(x, g, w, *, tm, tn, out_dtype):
    m, k = x.shape
    n = w.shape[1]
    tm = min(tm, m)
    tn = min(tn, n)
    assert m % tm == 0 and n % tn == 0
    return pl.pallas_call(
        functools.partial(_rms_matmul_kernel, row_chunk=32),
        grid=(m // tm, n // tn),
        in_specs=[pl.BlockSpec((tm, k), lambda i, j: (i, 0)),
                  pl.BlockSpec((1, k), lambda i, j: (0, 0)),
                  pl.BlockSpec((k, tn), lambda i, j: (0, j))],
        out_specs=pl.BlockSpec((tm, tn), lambda i, j: (i, j)),
        out_shape=jax.ShapeDtypeStruct((m, n), out_dtype),
        scratch_shapes=[pltpu.VMEM((tm, k), BF16)],
        compiler_params=_cparams("parallel", "arbitrary"),
        name="rms_matmul",
    )(x, g.reshape(1, k), w)


def _out_proj_kernel(gt_ref, gm_ref, wt_ref, wm_ref, x_ref, o_ref):
    acc = jnp.dot(gt_ref[...], wt_ref[...], preferred_element_type=F32)
    acc = acc + jnp.dot(gm_ref[...], wm_ref[...], preferred_element_type=F32)
    o_ref[...] = x_ref[...] + acc


def _out_proj(gt, gm, wt, wm, x, *, tm, tn):
    m, n = x.shape
    tm = min(tm, m)
    assert m % tm == 0 and n % tn == 0
    return pl.pallas_call(
        _out_proj_kernel,
        grid=(m // tm, n // tn),
        in_specs=[pl.BlockSpec((tm, gt.shape[1]), lambda i, j: (i, 0)),
                  pl.BlockSpec((tm, gm.shape[1]), lambda i, j: (i, 0)),
                  pl.BlockSpec((wt.shape[0], tn), lambda i, j: (0, j)),
                  pl.BlockSpec((wm.shape[0], tn), lambda i, j: (0, j)),
                  pl.BlockSpec((tm, tn), lambda i, j: (i, j))],
        out_specs=pl.BlockSpec((tm, tn), lambda i, j: (i, j)),
        out_shape=jax.ShapeDtypeStruct((m, n), F32),
        compiler_params=_cparams("parallel", "arbitrary"),
        name="out_proj",
    )(gt, gm, wt, wm, x)


def _rmsnorm_kernel(x_ref, g_ref, o_ref):
    xf = x_ref[...]
    ms = jnp.mean(xf * xf, axis=-1, keepdims=True)
    o_ref[...] = (xf * lax.rsqrt(ms + EPS)) * g_ref[...]


def _rmsnorm(x, g, *, tm):
    m, k = x.shape
    tm = min(tm, m)
    return pl.pallas_call(
        _rmsnorm_kernel,
        grid=(m // tm,),
        in_specs=[pl.BlockSpec((tm, k), lambda i: (i, 0)),
                  pl.BlockSpec((1, k), lambda i: (0, 0))],
        out_specs=pl.BlockSpec((tm, k), lambda i: (i, 0)),
        out_shape=jax.ShapeDtypeStruct((m, k), F32),
        compiler_params=_cparams("parallel"),
        name="final_rmsnorm",
    )(x, g.reshape(1, k))


def _mem_attn_kernel(q_ref, z_ref, mk_ref, mv_ref, o_ref):
    scale = MEM_HEAD_DIM ** -0.5
    for h in range(MEM_HEADS):
        sl = slice(h * MEM_HEAD_DIM, (h + 1) * MEM_HEAD_DIM)
        s = lax.dot_general(q_ref[:, sl], mk_ref[:, sl], (((1,), (1,)), ((), ())),
                            preferred_element_type=F32) * scale
        m = jnp.max(s, axis=-1, keepdims=True)
        p = jnp.exp(s - m)
        den = jnp.sum(p, axis=-1, keepdims=True)
        o = jnp.dot(p.astype(BF16), mv_ref[:, sl], preferred_element_type=F32) / den
        z = z_ref[:, sl].astype(F32)
        o_ref[:, sl] = (o * (z * _sigmoid(z))).astype(o_ref.dtype)


def _mem_attn(proj, kv, *, q_block, z_block, tq):
    b, t, _ = proj.shape
    tq = min(tq, t)
    return pl.pallas_call(
        _mem_attn_kernel,
        grid=(b, t // tq),
        in_specs=[pl.BlockSpec((None, tq, MEM_WIDTH), lambda bi, i: (bi, i, q_block)),
                  pl.BlockSpec((None, tq, MEM_WIDTH), lambda bi, i: (bi, i, z_block)),
                  pl.BlockSpec((None, N_MEM, MEM_WIDTH), lambda bi, i: (bi, 0, 0)),
                  pl.BlockSpec((None, N_MEM, MEM_WIDTH), lambda bi, i: (bi, 0, 1))],
        out_specs=pl.BlockSpec((None, tq, MEM_WIDTH), lambda bi, i: (bi, i, 0)),
        out_shape=jax.ShapeDtypeStruct((b, t, MEM_WIDTH), BF16),
        compiler_params=_cparams("parallel", "parallel"),
        name="mem_attn",
    )(proj, proj, kv, kv)


def _ssd_kernel(xbc_ref, z_ref, dt_ref, cw_ref, cb_ref, dtb_ref, alog_ref, dskip_ref, ng_ref,
                o_ref, halo_ref, xc_ref, state_ref):
    q = SSD_CHUNK
    c = pl.program_id(1)

    @pl.when(c == 0)
    def _():
        halo_ref[...] = jnp.zeros_like(halo_ref)
        state_ref[...] = jnp.zeros_like(state_ref)

    for t in range(SSD_CONV_DIM // LANES):
        cs = slice(t * LANES, (t + 1) * LANES)
        u = xbc_ref[:, cs].astype(F32)
        ext = jnp.concatenate([halo_ref[:, cs], u], axis=0)
        acc = cb_ref[:, cs] + cw_ref[SSD_CONV - 1:SSD_CONV, cs] * u
        for k in range(SSD_CONV - 1):
            shifted = pltpu.roll(ext, SSD_CONV - 1 - k, axis=0)[HALO_ROWS:, :]
            acc = acc + cw_ref[k:k + 1, cs] * shifted
        xc_ref[:, cs] = acc * _sigmoid(acc)
        halo_ref[:, cs] = u[q - HALO_ROWS:, :]

    dtr = dt_ref[...] + dtb_ref[...]
    dt = jnp.maximum(dtr, 0.0) + jnp.log1p(jnp.exp(-jnp.abs(dtr)))
    a = -jnp.exp(alog_ref[...])
    dta = dt * a
    row = lax.broadcasted_iota(jnp.int32, (q, q), 0)
    col = lax.broadcasted_iota(jnp.int32, (q, q), 1)
    causal = row >= col
    tri = jnp.where(causal, 1.0, 0.0).astype(BF16)
    hi, mid, lo = _split3(dta)
    a_cs = (jnp.dot(tri, hi, preferred_element_type=F32)
            + jnp.dot(tri, mid, preferred_element_type=F32)
            + jnp.dot(tri, lo, preferred_element_type=F32))
    a_cs_t = a_cs.T
    dt_t = dt.T
    a_last = a_cs[q - 1:q, :]
    e_acs = jnp.exp(a_cs)
    w_state = jnp.exp(a_last - a_cs) * dt
    chunk_decay = jnp.exp(a_last)
    lane = lax.broadcasted_iota(jnp.int32, (q, LANES), 1)
    first_head = lane < SSD_HEAD_DIM
    lane_n = lax.broadcasted_iota(jnp.int32, (SSD_STATE, LANES), 1)
    first_head_n = lane_n < SSD_HEAD_DIM

    def pair_cols(arr, h0):
        n = arr.shape[0]
        c0 = jnp.broadcast_to(arr[:, h0:h0 + 1], (n, LANES))
        c1 = jnp.broadcast_to(arr[:, h0 + 1:h0 + 2], (n, LANES))
        return jnp.where(first_head[:n], c0, c1)

    for g in range(SSD_GROUPS):
        b_off = TOK_WIDTH + g * SSD_STATE
        c_off = TOK_WIDTH + SSD_BC_COLS + g * SSD_STATE
        bm = xc_ref[:, b_off:b_off + SSD_STATE]
        cm = xc_ref[:, c_off:c_off + SSD_STATE].astype(BF16)
        bm_t = bm.T.astype(BF16)
        cb = lax.dot_general(cm, bm.astype(BF16), (((1,), (1,)), ((), ())),
                             preferred_element_type=F32)
        sumsq = jnp.zeros((q, 1), F32)
        gated = []
        for pr in range(SSD_PAIRS_PER_GROUP):
            h0 = g * SSD_HEADS_PER_GROUP + 2 * pr
            xs = slice(g * SSD_GROUP_COLS + pr * LANES, g * SSD_GROUP_COLS + (pr + 1) * LANES)
            ss = slice(pr * LANES, (pr + 1) * LANES)
            xp = xc_ref[:, xs]
            xpb = xp.astype(BF16)
            res = []
            for h in (h0, h0 + 1):
                seg = a_cs[:, h:h + 1] - a_cs_t[h:h + 1, :]
                decay = jnp.exp(jnp.where(causal, seg, NEG))
                mat = (cb * decay * dt_t[h:h + 1, :]).astype(BF16)
                res.append(jnp.dot(mat, xpb, preferred_element_type=F32))
            y = jnp.where(first_head, res[0], res[1])
            s_prev = state_ref[g, :, ss]
            y = y + jnp.dot(cm, s_prev.astype(BF16), preferred_element_type=F32) * pair_cols(e_acs, h0)
            y = y + dskip_ref[:, xs] * xp
            xw = (xp * pair_cols(w_state, h0)).astype(BF16)
            cd = jnp.where(first_head_n,
                           jnp.broadcast_to(chunk_decay[:, h0:h0 + 1], (SSD_STATE, LANES)),
                           jnp.broadcast_to(chunk_decay[:, h0 + 1:h0 + 2], (SSD_STATE, LANES)))
            state_ref[g, :, ss] = cd * s_prev + jnp.dot(bm_t, xw, preferred_element_type=F32)
            z = z_ref[:, xs].astype(F32)
            gz = y * (z * _sigmoid(z))
            sumsq = sumsq + jnp.sum(gz * gz, axis=-1, keepdims=True)
            gated.append(gz)
        rinv = lax.rsqrt(sumsq * (1.0 / SSD_GROUP_COLS) + EPS)
        for pr in range(SSD_PAIRS_PER_GROUP):
            xs = slice(g * SSD_GROUP_COLS + pr * LANES, g * SSD_GROUP_COLS + (pr + 1) * LANES)
            o_ref[:, xs] = ((gated[pr] * rinv) * ng_ref[:, xs]).astype(o_ref.dtype)


def _ssd_mixer(proj, dt_raw, conv_w, conv_b, dt_bias, a_log, d_skip, ssd_norm_g):
    b, t, _ = proj.shape
    q = SSD_CHUNK
    pad = LANES - SSD_HEADS
    full = lambda bi, ci: (0, 0)
    return pl.pallas_call(
        _ssd_kernel,
        grid=(b, t // q),
        in_specs=[pl.BlockSpec((None, q, SSD_CONV_DIM), lambda bi, ci: (bi, ci, 0)),
                  pl.BlockSpec((None, q, TOK_WIDTH), lambda bi, ci: (bi, ci, 2)),
                  pl.BlockSpec((None, q, LANES), lambda bi, ci: (bi, ci, 0)),
                  pl.BlockSpec((SSD_CONV, SSD_CONV_DIM), full),
                  pl.BlockSpec((1, SSD_CONV_DIM), full),
                  pl.BlockSpec((1, LANES), full),
                  pl.BlockSpec((1, LANES), full),
                  pl.BlockSpec((1, TOK_WIDTH), full),
                  pl.BlockSpec((1, TOK_WIDTH), full)],
        out_specs=pl.BlockSpec((None, q, TOK_WIDTH), lambda bi, ci: (bi, ci, 0)),
        out_shape=jax.ShapeDtypeStruct((b, t, TOK_WIDTH), BF16),
        scratch_shapes=[pltpu.VMEM((HALO_ROWS, SSD_CONV_DIM), F32),
                        pltpu.VMEM((q, SSD_CONV_DIM), F32),
                        pltpu.VMEM((SSD_GROUPS, SSD_STATE, SSD_GROUP_COLS), F32)],
        compiler_params=_cparams("parallel", "arbitrary"),
        name="ssd_mixer",
    )(proj, proj, dt_raw, conv_w, conv_b.reshape(1, -1),
      jnp.pad(dt_bias, (0, pad)).reshape(1, LANES), jnp.pad(a_log, (0, pad)).reshape(1, LANES),
      jnp.repeat(d_skip, SSD_HEAD_DIM).reshape(1, TOK_WIDTH), ssd_norm_g.reshape(1, TOK_WIDTH))


def _dil_attn_kernel(slope_ref, q_ref, kp_ref, kc_ref, vp_ref, vc_ref, o_ref, lse_ref):
    c = ATTN_BLOCK
    n = pl.program_id(2)
    scale = ATTN_HEAD_DIM ** -0.5
    row = lax.broadcasted_iota(jnp.int32, (c, c), 0)
    col = lax.broadcasted_iota(jnp.int32, (c, c), 1)
    rel_c = (row - col).astype(F32)
    rel_p = rel_c + float(c)
    valid_c = row >= col
    valid_p = col >= row
    prev_pen = jnp.where(n > 0, 0.0, NEG)
    lane = lax.broadcasted_iota(jnp.int32, (c, LANES), 1)
    lse_tile = jnp.zeros((c, LANES), F32)
    nt = (((1,), (1,)), ((), ()))
    for h in range(ATTN_HEADS):
        sl = slice(h * ATTN_HEAD_DIM, (h + 1) * ATTN_HEAD_DIM)
        slope = slope_ref[h]
        qh = q_ref[:, sl]
        s_p = lax.dot_general(qh, kp_ref[:, sl], nt, preferred_element_type=F32) * scale - slope * rel_p
        s_c = lax.dot_general(qh, kc_ref[:, sl], nt, preferred_element_type=F32) * scale - slope * rel_c
        s_p = jnp.where(valid_p, s_p + prev_pen, NEG)
        s_c = jnp.where(valid_c, s_c, NEG)
        m = jnp.maximum(jnp.max(s_p, axis=-1, keepdims=True), jnp.max(s_c, axis=-1, keepdims=True))
        p_p = jnp.exp(s_p - m)
        p_c = jnp.exp(s_c - m)
        den = jnp.sum(p_p, axis=-1, keepdims=True) + jnp.sum(p_c, axis=-1, keepdims=True)
        o = (jnp.dot(p_p.astype(BF16), vp_ref[:, sl], preferred_element_type=F32)
             + jnp.dot(p_c.astype(BF16), vc_ref[:, sl], preferred_element_type=F32))
        o_ref[:, sl] = (o / den).astype(o_ref.dtype)
        lse_tile = jnp.where(lane == h, m + jnp.log(den), lse_tile)
    lse_ref[...] = lse_tile


def _dil_attn(qkv, slopes_d, col_base):
    b, d, n_sub, _ = qkv.shape
    c = ATTN_BLOCK
    assert n_sub % c == 0
    blk = (None, None, c, TOK_WIDTH)
    cur = lambda off: (lambda bi, r, n, *_: (bi, r, n, col_base + off))
    prev = lambda off: (lambda bi, r, n, *_: (bi, r, jnp.maximum(n - 1, 0), col_base + off))
    out_map = lambda bi, r, n, *_: (bi, r, n, 0)
    return pl.pallas_call(
        _dil_attn_kernel,
        grid_spec=pltpu.PrefetchScalarGridSpec(
            num_scalar_prefetch=1,
            grid=(b, d, n_sub // c),
            in_specs=[pl.BlockSpec(blk, cur(0)),
                      pl.BlockSpec(blk, prev(1)), pl.BlockSpec(blk, cur(1)),
                      pl.BlockSpec(blk, prev(2)), pl.BlockSpec(blk, cur(2))],
            out_specs=[pl.BlockSpec(blk, out_map),
                       pl.BlockSpec((None, None, c, LANES), out_map)]),
        out_shape=[jax.ShapeDtypeStruct((b, d, n_sub, TOK_WIDTH), BF16),
                   jax.ShapeDtypeStruct((b, d, n_sub, LANES), F32)],
        compiler_params=_cparams("parallel", "parallel", "arbitrary"),
        name="dil_attn",
    )(slopes_d, qkv, qkv, qkv, qkv, qkv)


def _attn_mix_kernel(o0_ref, o1_ref, o2_ref, l0_ref, l1_ref, l2_ref, z_ref, out_ref):
    tm = out_ref.shape[0]
    l0, l1, l2 = l0_ref[...], l1_ref[...], l2_ref[...]
    mx = jnp.maximum(jnp.maximum(l0, l1), l2)
    e0, e1, e2 = jnp.exp(l0 - mx), jnp.exp(l1 - mx), jnp.exp(l2 - mx)
    inv = 1.0 / (e0 + e1 + e2)
    w0, w1, w2 = e0 * inv, e1 * inv, e2 * inv
    for h in range(ATTN_HEADS):
        sl = slice(h * ATTN_HEAD_DIM, (h + 1) * ATTN_HEAD_DIM)
        bc = lambda w: jnp.broadcast_to(w[:, h:h + 1], (tm, ATTN_HEAD_DIM))
        y = (bc(w0) * o0_ref[:, sl].astype(F32) + bc(w1) * o1_ref[:, sl].astype(F32)
             + bc(w2) * o2_ref[:, sl].astype(F32))
        z = z_ref[:, sl].astype(F32)
        out_ref[:, sl] = (y * (z * _sigmoid(z))).astype(out_ref.dtype)


def _attn_mix(os_, ls_, proj, *, z_block, tm):
    b, t, _ = proj.shape
    tm = min(tm, t)
    tok = pl.BlockSpec((None, tm, TOK_WIDTH), lambda bi, i: (bi, i, 0))
    lse = pl.BlockSpec((None, tm, LANES), lambda bi, i: (bi, i, 0))
    return pl.pallas_call(
        _attn_mix_kernel,
        grid=(b, t // tm),
        in_specs=[tok, tok, tok, lse, lse, lse,
                  pl.BlockSpec((None, tm, TOK_WIDTH), lambda bi, i: (bi, i, z_block))],
        out_specs=tok,
        out_shape=jax.ShapeDtypeStruct((b, t, TOK_WIDTH), BF16),
        compiler_params=_cparams("parallel", "parallel"),
        name="attn_mix",
    )(*os_, *ls_, proj)


def _mem_kv(mem, mem_norm_g, w_mem_kv):
    b, n, dm = mem.shape
    kv = _rms_matmul(mem.reshape(b * n, dm), mem_norm_g, w_mem_kv.astype(BF16),
                     tm=1024, tn=1024, out_dtype=BF16)
    return kv.reshape(b, n, 2 * MEM_WIDTH)


def _finish_layer(x2, gated_tok, gated_mem, w_out):
    m = x2.shape[0]
    wt = w_out[:TOK_WIDTH].astype(BF16)
    wm = w_out[TOK_WIDTH:].astype(BF16)
    return _out_proj(gated_tok.reshape(m, TOK_WIDTH), gated_mem.reshape(m, MEM_WIDTH), wt, wm, x2,
                     tm=1024, tn=512)


def _ssd_layer(x2, bt, kv, norm_g, w_in, conv_w, conv_b, dt_bias, a_log, d_skip, ssd_norm_g, w_out):
    b, t = bt
    o_dt = SSD_CONV_DIM
    o_q = o_dt + SSD_HEADS
    o_z = o_q + MEM_WIDTH
    w_main = jnp.concatenate([w_in[:, :o_dt], w_in[:, o_q:]], axis=1).astype(BF16)
    w_dt = jnp.pad(w_in[:, o_dt:o_q], ((0, 0), (0, LANES - SSD_HEADS))).astype(BF16)
    proj = _rms_matmul(x2, norm_g, w_main, tm=1024, tn=1024, out_dtype=BF16).reshape(b, t, -1)
    dt_raw = _rms_matmul(x2, norm_g, w_dt, tm=1024, tn=LANES, out_dtype=F32).reshape(b, t, LANES)
    gated_tok = _ssd_mixer(proj, dt_raw, conv_w, conv_b, dt_bias, a_log, d_skip, ssd_norm_g)
    gated_mem = _mem_attn(proj, kv, q_block=SSD_CONV_DIM // MEM_WIDTH,
                          z_block=(SSD_CONV_DIM + MEM_WIDTH + TOK_WIDTH) // MEM_WIDTH, tq=512)
    return _finish_layer(x2, gated_tok, gated_mem, w_out)


def _attn_layer(x2, bt, kv, norm_g, w_in, w_out):
    b, t = bt
    o_q = N_DIL * ATTN_GROUP_COLS
    o_z = o_q + MEM_WIDTH
    w_main = jnp.concatenate([w_in[:, :o_q], w_in[:, o_z:o_z + TOK_WIDTH], w_in[:, o_q:o_z],
                              w_in[:, o_z + TOK_WIDTH:]], axis=1).astype(BF16)
    proj = _rms_matmul(x2, norm_g, w_main, tm=1024, tn=1024, out_dtype=BF16).reshape(b, t, -1)
    slopes = jnp.exp2(-ALIBI_MAX_EXP * jnp.arange(1, N_DIL * ATTN_HEADS + 1, dtype=F32) / (N_DIL * ATTN_HEADS))
    slopes = slopes.reshape(N_DIL, ATTN_HEADS)
    outs, lses = [], []
    for g, (window, d) in enumerate(DILATED_GROUPS):
        assert window // d == ATTN_BLOCK
        n_sub = t // d
        if d == 1:
            qkv = proj.reshape(b, 1, t, -1)
            col_base = 3 * g
        else:
            qkv = proj[:, :, g * ATTN_GROUP_COLS:(g + 1) * ATTN_GROUP_COLS]
            qkv = qkv.reshape(b, n_sub, d, ATTN_GROUP_COLS).transpose(0, 2, 1, 3)
            col_base = 0
        o, lse = _dil_attn(qkv, slopes[g] * float(d), col_base)
        if d > 1:
            o = o.transpose(0, 2, 1, 3)
            lse = lse.transpose(0, 2, 1, 3)
        outs.append(o.reshape(b, t, TOK_WIDTH))
        lses.append(lse.reshape(b, t, LANES))
    gated_tok = _attn_mix(outs, lses, proj, z_block=o_q // TOK_WIDTH, tm=256)
    gated_mem = _mem_attn(proj, kv, q_block=(o_q + TOK_WIDTH) // MEM_WIDTH,
                          z_block=(o_q + TOK_WIDTH + MEM_WIDTH) // MEM_WIDTH, tq=512)
    return _finish_layer(x2, gated_tok, gated_mem, w_out)


def kernel(x, mem, mem_norm_g, final_norm_g, norm_g_0, w_in_0, conv_w_0, conv_b_0, dt_bias_0, a_log_0, d_skip_0, ssd_norm_g_0, w_mem_kv_0, w_out_0, norm_g_1, w_in_1, w_mem_kv_1, w_out_1, norm_g_2, w_in_2, conv_w_2, conv_b_2, dt_bias_2, a_log_2, d_skip_2, ssd_norm_g_2, w_mem_kv_2, w_out_2, norm_g_3, w_in_3, w_mem_kv_3, w_out_3):
    b, t, dm = x.shape
    bt = (b, t)
    x2 = x.reshape(b * t, dm)
    x2 = _ssd_layer(x2, bt, _mem_kv(mem, mem_norm_g, w_mem_kv_0), norm_g_0, w_in_0, conv_w_0, conv_b_0,
                    dt_bias_0, a_log_0, d_skip_0, ssd_norm_g_0, w_out_0)
    x2 = _attn_layer(x2, bt, _mem_kv(mem, mem_norm_g, w_mem_kv_1), norm_g_1, w_in_1, w_out_1)
    x2 = _ssd_layer(x2, bt, _mem_kv(mem, mem_norm_g, w_mem_kv_2), norm_g_2, w_in_2, conv_w_2, conv_b_2,
                    dt_bias_2, a_log_2, d_skip_2, ssd_norm_g_2, w_out_2)
    x2 = _attn_layer(x2, bt, _mem_kv(mem, mem_norm_g, w_mem_kv_3), norm_g_3, w_in_3, w_out_3)
    return _rmsnorm(x2, final_norm_g, tm=512).reshape(b, t, dm)
```

```python
import functools

import jax
import jax.numpy as jnp
from jax import lax
from jax.experimental import pallas as pl
from jax.experimental.pallas import tpu as pltpu

F32 = jnp.float32
BF16 = jnp.bfloat16
EPS = 1e-6
NEG = -1e30
LOG2E = 1.4426950408889634

LANES = 128
D_MODEL = 2048
N_MEM = 256
MIX_WIDTH = 2 * D_MODEL
MEM_WIDTH = MIX_WIDTH // 4
TOK_WIDTH = MIX_WIDTH - MEM_WIDTH
MEM_HEADS = 4
MEM_HEAD_DIM = MEM_WIDTH // MEM_HEADS

SSD_HEAD_DIM = 64
SSD_HEADS = TOK_WIDTH // SSD_HEAD_DIM
SSD_GROUPS = 8
SSD_HEADS_PER_GROUP = SSD_HEADS // SSD_GROUPS
SSD_GROUP_COLS = SSD_HEADS_PER_GROUP * SSD_HEAD_DIM
SSD_PAIRS_PER_GROUP = SSD_GROUP_COLS // LANES
SSD_STATE = 128
SSD_CONV = 4
SSD_CHUNK = 128
SSD_BC_COLS = SSD_GROUPS * SSD_STATE
SSD_CONV_DIM = TOK_WIDTH + 2 * SSD_BC_COLS
HALO_ROWS = 8

ATTN_HEAD_DIM = 128
ATTN_HEADS = TOK_WIDTH // ATTN_HEAD_DIM
DILATED_GROUPS = ((128, 1), (512, 4), (2048, 16))
N_DIL = len(DILATED_GROUPS)
ATTN_GROUP_COLS = 3 * TOK_WIDTH
ATTN_BLOCK = 128
ALIBI_MAX_EXP = 8.0

VMEM_LIMIT = 56 * 1024 * 1024


def _cparams(*sem):
    return pltpu.CompilerParams(dimension_semantics=sem, vmem_limit_bytes=VMEM_LIMIT)


def _silu(v):
    hv = 0.5 * v
    return hv + hv * jnp.tanh(hv)


def _split3(v):
    hi = v.astype(BF16)
    r1 = v - hi.astype(F32)
    mid = r1.astype(BF16)
    lo = (r1 - mid.astype(F32)).astype(BF16)
    return hi, mid, lo


PERM_CHUNK = 256


def _rms_matmul_kernel(x_ref, g_ref, w_ref, o_ref, h_ref, *hn_refs, row_chunk, dil):
    tm = x_ref.shape[0]
    sub = tm // dil

    @pl.when(pl.program_id(2) == 0)
    def _():
        norm_ref = h_ref if dil == 1 else hn_refs[0]

        def body(i, carry):
            r = pl.multiple_of(i * row_chunk, row_chunk)
            xf = x_ref[pl.ds(r, row_chunk), :]
            ms = jnp.mean(xf * xf, axis=-1, keepdims=True)
            norm_ref[pl.ds(r, row_chunk), :] = ((xf * lax.rsqrt(ms + EPS)) * g_ref[...]).astype(BF16)
            return carry
        lax.fori_loop(0, tm // row_chunk, body, 0)

        if dil > 1:
            csub = PERM_CHUNK // dil
            row = lax.broadcasted_iota(jnp.int32, (PERM_CHUNK, PERM_CHUNK), 0)
            col = lax.broadcasted_iota(jnp.int32, (PERM_CHUNK, PERM_CHUNK), 1)
            perm = jnp.where((row % csub) * dil + row // csub == col, 1.0, 0.0).astype(BF16)
            for c in range(tm // PERM_CHUNK):
                hp = jnp.dot(perm, norm_ref[c * PERM_CHUNK:(c + 1) * PERM_CHUNK, :],
                             preferred_element_type=F32).astype(BF16)
                for r in range(dil):
                    h_ref[r * sub + c * csub:r * sub + (c + 1) * csub, :] = hp[r * csub:(r + 1) * csub, :]

    res = jnp.dot(h_ref[...], w_ref[...], preferred_element_type=F32).astype(o_ref.dtype)
    o_ref[...] = res.reshape(o_ref.shape)


def _rms_matmul(x, g, w, *, tm, tn, out_dtype, dil=1):
    b, t, k = x.shape
    n = w.shape[1]
    tm = min(tm, t)
    tn = min(tn, n)
    row_chunk = 32
    assert t % tm == 0 and n % tn == 0 and tm % row_chunk == 0 and (dil == 1 or tm % PERM_CHUNK == 0)
    scratch = [pltpu.VMEM((tm, k), BF16)] * (1 if dil == 1 else 2)
    return pl.pallas_call(
        functools.partial(_rms_matmul_kernel, row_chunk=row_chunk, dil=dil),
        grid=(b, t // tm, n // tn),
        in_specs=[pl.BlockSpec((None, tm, k), lambda bi, i, j: (bi, i, 0)),
                  pl.BlockSpec((1, k), lambda bi, i, j: (0, 0)),
                  pl.BlockSpec((k, tn), lambda bi, i, j: (0, j))],
        out_specs=pl.BlockSpec((None, dil, tm // dil, tn), lambda bi, i, j: (bi, 0, i, j)),
        out_shape=jax.ShapeDtypeStruct((b, dil, t // dil, n), out_dtype),
        scratch_shapes=scratch,
        compiler_params=_cparams("parallel", "parallel", "arbitrary"),
        name="rms_matmul",
    )(x, g.reshape(1, k), w)


def _out_proj_kernel(gt_ref, gm_ref, wt_ref, wm_ref, x_ref, o_ref):
    acc = jnp.dot(gt_ref[...], wt_ref[...], preferred_element_type=F32)
    acc = acc + jnp.dot(gm_ref[...], wm_ref[...], preferred_element_type=F32)
    o_ref[...] = x_ref[...] + acc


def _out_proj(gt, gm, wt, wm, x, *, tm, tn):
    m, n = x.shape
    tm = min(tm, m)
    assert m % tm == 0 and n % tn == 0
    return pl.pallas_call(
        _out_proj_kernel,
        grid=(m // tm, n // tn),
        in_specs=[pl.BlockSpec((tm, gt.shape[1]), lambda i, j: (i, 0)),
                  pl.BlockSpec((tm, gm.shape[1]), lambda i, j: (i, 0)),
                  pl.BlockSpec((wt.shape[0], tn), lambda i, j: (0, j)),
                  pl.BlockSpec((wm.shape[0], tn), lambda i, j: (0, j)),
                  pl.BlockSpec((tm, tn), lambda i, j: (i, j))],
        out_specs=pl.BlockSpec((tm, tn), lambda i, j: (i, j)),
        out_shape=jax.ShapeDtypeStruct((m, n), F32),
        compiler_params=_cparams("parallel", "arbitrary"),
        name="out_proj",
    )(gt, gm, wt, wm, x)


def _rmsnorm_kernel(x_ref, g_ref, o_ref):
    xf = x_ref[...]
    ms = jnp.mean(xf * xf, axis=-1, keepdims=True)
    o_ref[...] = (xf * lax.rsqrt(ms + EPS)) * g_ref[...]


def _rmsnorm(x, g, *, tm):
    m, k = x.shape
    tm = min(tm, m)
    return pl.pallas_call(
        _rmsnorm_kernel,
        grid=(m // tm,),
        in_specs=[pl.BlockSpec((tm, k), lambda i: (i, 0)),
                  pl.BlockSpec((1, k), lambda i: (0, 0))],
        out_specs=pl.BlockSpec((tm, k), lambda i: (i, 0)),
        out_shape=jax.ShapeDtypeStruct((m, k), F32),
        compiler_params=_cparams("parallel"),
        name="final_rmsnorm",
    )(x, g.reshape(1, k))


def _mem_attn_kernel(q_ref, z_ref, mk_ref, mv_ref, o_ref):
    scale = MEM_HEAD_DIM ** -0.5
    for h in range(MEM_HEADS):
        sl = slice(h * MEM_HEAD_DIM, (h + 1) * MEM_HEAD_DIM)
        s = lax.dot_general(q_ref[:, sl], mk_ref[:, sl], (((1,), (1,)), ((), ())),
                            preferred_element_type=F32) * scale
        m = jnp.max(s, axis=-1, keepdims=True)
        p = jnp.exp(s - m)
        den = jnp.sum(p, axis=-1, keepdims=True)
        o = jnp.dot(p.astype(BF16), mv_ref[:, sl], preferred_element_type=F32) / den
        z = z_ref[:, sl].astype(F32)
        o_ref[:, sl] = (o * _silu(z)).astype(o_ref.dtype)


def _mem_attn(proj, kv, *, q_block, z_block, tq):
    b, t, _ = proj.shape
    tq = min(tq, t)
    return pl.pallas_call(
        _mem_attn_kernel,
        grid=(b, t // tq),
        in_specs=[pl.BlockSpec((None, tq, MEM_WIDTH), lambda bi, i: (bi, i, q_block)),
                  pl.BlockSpec((None, tq, MEM_WIDTH), lambda bi, i: (bi, i, z_block)),
                  pl.BlockSpec((None, N_MEM, MEM_WIDTH), lambda bi, i: (bi, 0, 0)),
                  pl.BlockSpec((None, N_MEM, MEM_WIDTH), lambda bi, i: (bi, 0, 1))],
        out_specs=pl.BlockSpec((None, tq, MEM_WIDTH), lambda bi, i: (bi, i, 0)),
        out_shape=jax.ShapeDtypeStruct((b, t, MEM_WIDTH), BF16),
        compiler_params=_cparams("parallel", "parallel"),
        name="mem_attn",
    )(proj, proj, kv, kv)


def _ssd_kernel(xbc_ref, z_ref, dt_ref, cw_ref, cb_ref, dtb_ref, alog_ref, dskip_ref, ng_ref,
                o_ref, halo_ref, xc_ref, state_ref):
    q = SSD_CHUNK
    c = pl.program_id(1)

    @pl.when(c == 0)
    def _():
        halo_ref[...] = jnp.zeros_like(halo_ref)
        state_ref[...] = jnp.zeros_like(state_ref)

    for t in range(SSD_CONV_DIM // LANES):
        cs = slice(t * LANES, (t + 1) * LANES)
        u = xbc_ref[:, cs].astype(F32)
        ext = jnp.concatenate([halo_ref[:, cs], u], axis=0)
        acc = cb_ref[:, cs] + cw_ref[SSD_CONV - 1:SSD_CONV, cs] * u
        for k in range(SSD_CONV - 1):
            shifted = pltpu.roll(ext, SSD_CONV - 1 - k, axis=0)[HALO_ROWS:, :]
            acc = acc + cw_ref[k:k + 1, cs] * shifted
        xc_ref[:, cs] = _silu(acc)
        halo_ref[:, cs] = u[q - HALO_ROWS:, :]

    dtr = dt_ref[...] + dtb_ref[...]
    dt = jnp.maximum(dtr, 0.0) + jnp.log1p(jnp.exp(-jnp.abs(dtr)))
    a = -jnp.exp(alog_ref[...])
    dta = dt * a
    row = lax.broadcasted_iota(jnp.int32, (q, q), 0)
    col = lax.broadcasted_iota(jnp.int32, (q, q), 1)
    causal = row >= col
    tri = jnp.where(causal, 1.0, 0.0).astype(BF16)
    hi, mid, lo = _split3(dta)
    a_cs = (jnp.dot(tri, hi, preferred_element_type=F32)
            + jnp.dot(tri, mid, preferred_element_type=F32)
            + jnp.dot(tri, lo, preferred_element_type=F32))
    a_cs_t = a_cs.T
    dt_t = dt.T
    a_last = a_cs[q - 1:q, :]
    e_acs = jnp.exp(a_cs)
    w_state = jnp.exp(a_last - a_cs) * dt
    chunk_decay = jnp.exp(a_last)
    lane = lax.broadcasted_iota(jnp.int32, (q, LANES), 1)
    first_head = lane < SSD_HEAD_DIM
    lane_n = lax.broadcasted_iota(jnp.int32, (SSD_STATE, LANES), 1)
    first_head_n = lane_n < SSD_HEAD_DIM

    def pair_cols(arr, h0):
        n = arr.shape[0]
        c0 = jnp.broadcast_to(arr[:, h0:h0 + 1], (n, LANES))
        c1 = jnp.broadcast_to(arr[:, h0 + 1:h0 + 2], (n, LANES))
        return jnp.where(first_head[:n], c0, c1)

    for g in range(SSD_GROUPS):
        b_off = TOK_WIDTH + g * SSD_STATE
        c_off = TOK_WIDTH + SSD_BC_COLS + g * SSD_STATE
        bm = xc_ref[:, b_off:b_off + SSD_STATE]
        cm = xc_ref[:, c_off:c_off + SSD_STATE].astype(BF16)
        bm_t = bm.T.astype(BF16)
        cb = lax.dot_general(cm, bm.astype(BF16), (((1,), (1,)), ((), ())),
                             preferred_element_type=F32)
        sumsq = jnp.zeros((q, 1), F32)
        gated = []
        for pr in range(SSD_PAIRS_PER_GROUP):
            h0 = g * SSD_HEADS_PER_GROUP + 2 * pr
            xs = slice(g * SSD_GROUP_COLS + pr * LANES, g * SSD_GROUP_COLS + (pr + 1) * LANES)
            ss = slice(pr * LANES, (pr + 1) * LANES)
            xp = xc_ref[:, xs]
            xpb = xp.astype(BF16)
            res = []
            for h in (h0, h0 + 1):
                seg = a_cs[:, h:h + 1] - a_cs_t[h:h + 1, :]
                decay = jnp.exp(jnp.where(causal, seg, NEG))
                mat = (cb * decay * dt_t[h:h + 1, :]).astype(BF16)
                res.append(jnp.dot(mat, xpb, preferred_element_type=F32))
            y = jnp.where(first_head, res[0], res[1])
            s_prev = state_ref[g, :, ss]
            y = y + jnp.dot(cm, s_prev.astype(BF16), preferred_element_type=F32) * pair_cols(e_acs, h0)
            y = y + dskip_ref[:, xs] * xp
            xw = (xp * pair_cols(w_state, h0)).astype(BF16)
            cd = jnp.where(first_head_n,
                           jnp.broadcast_to(chunk_decay[:, h0:h0 + 1], (SSD_STATE, LANES)),
                           jnp.broadcast_to(chunk_decay[:, h0 + 1:h0 + 2], (SSD_STATE, LANES)))
            state_ref[g, :, ss] = cd * s_prev + jnp.dot(bm_t, xw, preferred_element_type=F32)
            z = z_ref[:, xs].astype(F32)
            gz = y * _silu(z)
            sumsq = sumsq + jnp.sum(gz * gz, axis=-1, keepdims=True)
            gated.append(gz)
        rinv = lax.rsqrt(sumsq * (1.0 / SSD_GROUP_COLS) + EPS)
        for pr in range(SSD_PAIRS_PER_GROUP):
            xs = slice(g * SSD_GROUP_COLS + pr * LANES, g * SSD_GROUP_COLS + (pr + 1) * LANES)
            o_ref[:, xs] = ((gated[pr] * rinv) * ng_ref[:, xs]).astype(o_ref.dtype)


def _ssd_mixer(proj, dt_raw, conv_w, conv_b, dt_bias, a_log, d_skip, ssd_norm_g):
    b, t, _ = proj.shape
    q = SSD_CHUNK
    pad = LANES - SSD_HEADS
    full = lambda bi, ci: (0, 0)
    return pl.pallas_call(
        _ssd_kernel,
        grid=(b, t // q),
        in_specs=[pl.BlockSpec((None, q, SSD_CONV_DIM), lambda bi, ci: (bi, ci, 0)),
                  pl.BlockSpec((None, q, TOK_WIDTH), lambda bi, ci: (bi, ci, 2)),
                  pl.BlockSpec((None, q, LANES), lambda bi, ci: (bi, ci, 0)),
                  pl.BlockSpec((SSD_CONV, SSD_CONV_DIM), full),
                  pl.BlockSpec((1, SSD_CONV_DIM), full),
                  pl.BlockSpec((1, LANES), full),
                  pl.BlockSpec((1, LANES), full),
                  pl.BlockSpec((1, TOK_WIDTH), full),
                  pl.BlockSpec((1, TOK_WIDTH), full)],
        out_specs=pl.BlockSpec((None, q, TOK_WIDTH), lambda bi, ci: (bi, ci, 0)),
        out_shape=jax.ShapeDtypeStruct((b, t, TOK_WIDTH), BF16),
        scratch_shapes=[pltpu.VMEM((HALO_ROWS, SSD_CONV_DIM), F32),
                        pltpu.VMEM((q, SSD_CONV_DIM), F32),
                        pltpu.VMEM((SSD_GROUPS, SSD_STATE, SSD_GROUP_COLS), F32)],
        compiler_params=_cparams("parallel", "arbitrary"),
        name="ssd_mixer",
    )(proj, proj, dt_raw, conv_w, conv_b.reshape(1, -1),
      jnp.pad(dt_bias, (0, pad)).reshape(1, LANES), jnp.pad(a_log, (0, pad)).reshape(1, LANES),
      jnp.repeat(d_skip, SSD_HEAD_DIM).reshape(1, TOK_WIDTH), ssd_norm_g.reshape(1, TOK_WIDTH))


ATTN_SLOTS = 4


def _dil_attn_kernel(slope_ref, q_ref, kp_ref, kc_ref, vp_ref, vc_ref, o_ref, lse_ref,
                     bias_ref, s_ref, p_ref):
    c = ATTN_BLOCK
    n = pl.program_id(2)

    @pl.when((pl.program_id(0) == 0) & (pl.program_id(1) == 0) & (n == 0))
    def _():
        row = lax.broadcasted_iota(jnp.int32, (c, 2 * c), 0)
        col = lax.broadcasted_iota(jnp.int32, (c, 2 * c), 1)
        rel = row + c - col
        band = (rel >= 0) & (rel <= c)
        band_first = band & (col >= c)
        relf = rel.astype(F32)
        for h in range(ATTN_HEADS):
            alibi = -slope_ref[h] * relf
            bias_ref[0, h] = jnp.where(band, alibi, NEG)
            bias_ref[1, h] = jnp.where(band_first, alibi, NEG)

    table = jnp.where(n == 0, 1, 0)
    lane = lax.broadcasted_iota(jnp.int32, (c, LANES), 1)
    nt = (((1,), (1,)), ((), ()))
    lse_ref[...] = jnp.zeros_like(lse_ref)

    def scores(h):
        sl = slice(h * ATTN_HEAD_DIM, (h + 1) * ATTN_HEAD_DIM)
        kw = jnp.concatenate([kp_ref[:, sl], kc_ref[:, sl]], axis=0)
        s = lax.dot_general(q_ref[:, sl], kw, nt, preferred_element_type=F32)
        s_ref[h % ATTN_SLOTS] = s + bias_ref[table, h]

    def softmax(h):
        s = s_ref[h % ATTN_SLOTS]
        m = jnp.max(s, axis=-1, keepdims=True)
        p = jnp.exp2(s - m)
        den = jnp.sum(p, axis=-1, keepdims=True)
        p_ref[h % ATTN_SLOTS] = (p * (1.0 / den)).astype(BF16)
        lse_ref[...] = jnp.where(lane == h, m + jnp.log2(den), lse_ref[...])

    def values(h):
        sl = slice(h * ATTN_HEAD_DIM, (h + 1) * ATTN_HEAD_DIM)
        vw = jnp.concatenate([vp_ref[:, sl], vc_ref[:, sl]], axis=0)
        o_ref[:, sl] = jnp.dot(p_ref[h % ATTN_SLOTS], vw, preferred_element_type=F32).astype(o_ref.dtype)

    for i in range(ATTN_HEADS + 2):
        if i < ATTN_HEADS:
            scores(i)
        if 1 <= i <= ATTN_HEADS:
            softmax(i - 1)
        if i >= 2:
            values(i - 2)


def _dil_attn(qkv, slopes_d):
    b, d, n_sub, _ = qkv.shape
    c = ATTN_BLOCK
    assert n_sub % c == 0
    blk = (None, None, c, TOK_WIDTH)
    cur = lambda off: (lambda bi, r, n, *_: (bi, r, n, off))
    prev = lambda off: (lambda bi, r, n, *_: (bi, r, jnp.maximum(n - 1, 0), off))
    out_map = lambda bi, r, n, *_: (bi, r, n, 0)
    return pl.pallas_call(
        _dil_attn_kernel,
        grid_spec=pltpu.PrefetchScalarGridSpec(
            num_scalar_prefetch=1,
            grid=(b, d, n_sub // c),
            in_specs=[pl.BlockSpec(blk, cur(0)),
                      pl.BlockSpec(blk, prev(1)), pl.BlockSpec(blk, cur(1)),
                      pl.BlockSpec(blk, prev(2)), pl.BlockSpec(blk, cur(2))],
            out_specs=[pl.BlockSpec(blk, out_map),
                       pl.BlockSpec((None, None, c, LANES), out_map)],
            scratch_shapes=[pltpu.VMEM((2, ATTN_HEADS, c, 2 * c), F32),
                            pltpu.VMEM((ATTN_SLOTS, c, 2 * c), F32),
                            pltpu.VMEM((ATTN_SLOTS, c, 2 * c), BF16)]),
        out_shape=[jax.ShapeDtypeStruct((b, d, n_sub, TOK_WIDTH), BF16),
                   jax.ShapeDtypeStruct((b, d, n_sub, LANES), F32)],
        compiler_params=_cparams("arbitrary", "arbitrary", "arbitrary"),
        name="dil_attn",
    )(slopes_d, qkv, qkv, qkv, qkv, qkv)


def _attn_mix_kernel(o0_ref, o1_ref, o2_ref, l0_ref, l1_ref, l2_ref, z_ref, out_ref, *, dils):
    tm = out_ref.shape[0]
    row = lax.broadcasted_iota(jnp.int32, (tm, tm), 0)
    col = lax.broadcasted_iota(jnp.int32, (tm, tm), 1)

    def perm(d):
        sub = tm // d
        return jnp.where((row % d) * sub + row // d == col, 1.0, 0.0).astype(BF16)

    def stacked(ref, d):
        return ref[0] if d == 1 else jnp.concatenate([ref[r] for r in range(d)], axis=0)

    perms = [None if d == 1 else perm(d) for d in dils]
    o_refs = (o0_ref, o1_ref, o2_ref)
    lses = []
    for ref, d, pm in zip((l0_ref, l1_ref, l2_ref), dils, perms):
        l = stacked(ref, d)
        if pm is not None:
            l = sum(jnp.dot(pm, part, preferred_element_type=F32) for part in _split3(l))
        lses.append(l)
    mx = jnp.maximum(jnp.maximum(lses[0], lses[1]), lses[2])
    es = [jnp.exp2(l - mx) for l in lses]
    inv = 1.0 / (es[0] + es[1] + es[2])
    ws = [e * inv for e in es]
    for h in range(ATTN_HEADS):
        sl = slice(h * ATTN_HEAD_DIM, (h + 1) * ATTN_HEAD_DIM)
        y = None
        for ref, d, pm, w in zip(o_refs, dils, perms, ws):
            if d == 1:
                o = ref[0, :, sl].astype(F32)
            else:
                o = jnp.dot(pm, jnp.concatenate([ref[r, :, sl] for r in range(d)], axis=0),
                            preferred_element_type=F32)
            term = jnp.broadcast_to(w[:, h:h + 1], (tm, ATTN_HEAD_DIM)) * o
            y = term if y is None else y + term
        out_ref[:, sl] = (y * _silu(z_ref[:, sl].astype(F32))).astype(out_ref.dtype)


def _attn_mix(os_, ls_, rest, *, tm):
    b, t, _ = rest.shape
    tm = min(tm, t)
    dils = tuple(o.shape[1] for o in os_)
    grp = lambda d, w: pl.BlockSpec((None, d, tm // d, w), lambda bi, i: (bi, 0, i, 0))
    tok = pl.BlockSpec((None, tm, TOK_WIDTH), lambda bi, i: (bi, i, 0))
    return pl.pallas_call(
        functools.partial(_attn_mix_kernel, dils=dils),
        grid=(b, t // tm),
        in_specs=[grp(d, TOK_WIDTH) for d in dils] + [grp(d, LANES) for d in dils] + [tok],
        out_specs=tok,
        out_shape=jax.ShapeDtypeStruct((b, t, TOK_WIDTH), BF16),
        compiler_params=_cparams("parallel", "parallel"),
        name="attn_mix",
    )(*os_, *ls_, rest)


def _mem_kv(mem, mem_norm_g, w_mem_kv):
    b, n, dm = mem.shape
    kv = _rms_matmul(mem.reshape(1, b * n, dm), mem_norm_g, w_mem_kv.astype(BF16),
                     tm=1024, tn=1024, out_dtype=BF16)
    return kv.reshape(b, n, 2 * MEM_WIDTH)


def _finish_layer(x2, gated_tok, gated_mem, w_out):
    m = x2.shape[0]
    wt = w_out[:TOK_WIDTH].astype(BF16)
    wm = w_out[TOK_WIDTH:].astype(BF16)
    return _out_proj(gated_tok.reshape(m, TOK_WIDTH), gated_mem.reshape(m, MEM_WIDTH), wt, wm, x2,
                     tm=1024, tn=512)


def _ssd_layer(x2, bt, kv, norm_g, w_in, conv_w, conv_b, dt_bias, a_log, d_skip, ssd_norm_g, w_out):
    b, t = bt
    o_dt = SSD_CONV_DIM
    o_q = o_dt + SSD_HEADS
    o_z = o_q + MEM_WIDTH
    w_main = jnp.concatenate([w_in[:, :o_dt], w_in[:, o_q:]], axis=1).astype(BF16)
    w_dt = jnp.pad(w_in[:, o_dt:o_q], ((0, 0), (0, LANES - SSD_HEADS))).astype(BF16)
    x3 = x2.reshape(b, t, -1)
    proj = _rms_matmul(x3, norm_g, w_main, tm=1024, tn=1024, out_dtype=BF16).reshape(b, t, -1)
    dt_raw = _rms_matmul(x3, norm_g, w_dt, tm=1024, tn=LANES, out_dtype=F32).reshape(b, t, LANES)
    gated_tok = _ssd_mixer(proj, dt_raw, conv_w, conv_b, dt_bias, a_log, d_skip, ssd_norm_g)
    gated_mem = _mem_attn(proj, kv, q_block=SSD_CONV_DIM // MEM_WIDTH,
                          z_block=(SSD_CONV_DIM + MEM_WIDTH + TOK_WIDTH) // MEM_WIDTH, tq=512)
    return _finish_layer(x2, gated_tok, gated_mem, w_out)


def _attn_layer(x2, bt, kv, norm_g, w_in, w_out):
    b, t = bt
    o_q = N_DIL * ATTN_GROUP_COLS
    o_z = o_q + MEM_WIDTH
    x3 = x2.reshape(b, t, -1)
    w_rest = jnp.concatenate([w_in[:, o_z:o_z + TOK_WIDTH], w_in[:, o_q:o_z], w_in[:, o_z + TOK_WIDTH:]],
                             axis=1).astype(BF16)
    rest = _rms_matmul(x3, norm_g, w_rest, tm=1024, tn=1024, out_dtype=BF16).reshape(b, t, -1)
    slopes = jnp.exp2(-ALIBI_MAX_EXP * jnp.arange(1, N_DIL * ATTN_HEADS + 1, dtype=F32) / (N_DIL * ATTN_HEADS))
    slopes = slopes.reshape(N_DIL, ATTN_HEADS)
    q_scale = (ATTN_HEAD_DIM ** -0.5) * LOG2E
    outs, lses = [], []
    for g, (window, d) in enumerate(DILATED_GROUPS):
        assert window // d == ATTN_BLOCK
        w_g = w_in[:, g * ATTN_GROUP_COLS:(g + 1) * ATTN_GROUP_COLS]
        w_g = jnp.concatenate([w_g[:, :TOK_WIDTH] * q_scale, w_g[:, TOK_WIDTH:]], axis=1).astype(BF16)
        qkv = _rms_matmul(x3, norm_g, w_g, tm=1024, tn=1024, out_dtype=BF16, dil=d)
        o, lse = _dil_attn(qkv, slopes[g] * (float(d) * LOG2E))
        outs.append(o)
        lses.append(lse)
    gated_tok = _attn_mix(outs, lses, rest, tm=256)
    gated_mem = _mem_attn(rest, kv, q_block=TOK_WIDTH // MEM_WIDTH,
                          z_block=(TOK_WIDTH + MEM_WIDTH) // MEM_WIDTH, tq=512)
    return _finish_layer(x2, gated_tok, gated_mem, w_out)


def kernel(x, mem, mem_norm_g, final_norm_g, norm_g_0, w_in_0, conv_w_0, conv_b_0, dt_bias_0, a_log_0, d_skip_0, ssd_norm_g_0, w_mem_kv_0, w_out_0, norm_g_1, w_in_1, w_mem_kv_1, w_out_1, norm_g_2, w_in_2, conv_w_2, conv_b_2, dt_bias_2, a_log_2, d_skip_2, ssd_norm_g_2, w_mem_kv_2, w_out_2, norm_g_3, w_in_3, w_mem_kv_3, w_out_3):
    b, t, dm = x.shape
    bt = (b, t)
    x2 = x.reshape(b * t, dm)
    x2 = _ssd_layer(x2, bt, _mem_kv(mem, mem_norm_g, w_mem_kv_0), norm_g_0, w_in_0, conv_w_0, conv_b_0,
                    dt_bias_0, a_log_0, d_skip_0, ssd_norm_g_0, w_out_0)
    x2 = _attn_layer(x2, bt, _mem_kv(mem, mem_norm_g, w_mem_kv_1), norm_g_1, w_in_1, w_out_1)
    x2 = _ssd_layer(x2, bt, _mem_kv(mem, mem_norm_g, w_mem_kv_2), norm_g_2, w_in_2, conv_w_2, conv_b_2,
                    dt_bias_2, a_log_2, d_skip_2, ssd_norm_g_2, w_out_2)
    x2 = _attn_layer(x2, bt, _mem_kv(mem, mem_norm_g, w_mem_kv_3), norm_g_3, w_in_3, w_out_3)
    return _rmsnorm(x2, final_norm_g, tm=512).reshape(b, t, dm)
```

```python
import functools

import jax
import jax.numpy as jnp
from jax import lax
from jax.experimental import pallas as pl
from jax.experimental.pallas import tpu as pltpu

F32 = jnp.float32
BF16 = jnp.bfloat16
EPS = 1e-6
NEG = -1e30
LOG2E = 1.4426950408889634

LANES = 128
D_MODEL = 2048
N_MEM = 256
MIX_WIDTH = 2 * D_MODEL
MEM_WIDTH = MIX_WIDTH // 4
TOK_WIDTH = MIX_WIDTH - MEM_WIDTH
MEM_HEADS = 4
MEM_HEAD_DIM = MEM_WIDTH // MEM_HEADS

SSD_HEAD_DIM = 64
SSD_HEADS = TOK_WIDTH // SSD_HEAD_DIM
SSD_GROUPS = 8
SSD_HEADS_PER_GROUP = SSD_HEADS // SSD_GROUPS
SSD_GROUP_COLS = SSD_HEADS_PER_GROUP * SSD_HEAD_DIM
SSD_PAIRS_PER_GROUP = SSD_GROUP_COLS // LANES
SSD_STATE = 128
SSD_CONV = 4
SSD_CHUNK = 128
SSD_BC_COLS = SSD_GROUPS * SSD_STATE
SSD_CONV_DIM = TOK_WIDTH + 2 * SSD_BC_COLS
HALO_ROWS = 8

ATTN_HEAD_DIM = 128
ATTN_HEADS = TOK_WIDTH // ATTN_HEAD_DIM
DILATED_GROUPS = ((128, 1), (512, 4), (2048, 16))
N_DIL = len(DILATED_GROUPS)
ATTN_GROUP_COLS = 3 * TOK_WIDTH
ATTN_BLOCK = 128
ALIBI_MAX_EXP = 8.0

VMEM_LIMIT = 56 * 1024 * 1024


def _cparams(*sem):
    return pltpu.CompilerParams(dimension_semantics=sem, vmem_limit_bytes=VMEM_LIMIT)


def _silu(v):
    hv = 0.5 * v
    return hv + hv * jnp.tanh(hv)


def _split3(v):
    hi = v.astype(BF16)
    r1 = v - hi.astype(F32)
    mid = r1.astype(BF16)
    lo = (r1 - mid.astype(F32)).astype(BF16)
    return hi, mid, lo


ROW_TILE = 1024
PERM_CHUNK = 256
NORM_ROWS = 32


def _norm_kernel(x_ref, g_ref, *out_refs, dils):
    tm = x_ref.shape[0]
    nat_ref = out_refs[0]

    def body(i, carry):
        r = pl.multiple_of(i * NORM_ROWS, NORM_ROWS)
        xf = x_ref[pl.ds(r, NORM_ROWS), :]
        ms = jnp.mean(xf * xf, axis=-1, keepdims=True)
        nat_ref[pl.ds(r, NORM_ROWS), :] = ((xf * lax.rsqrt(ms + EPS)) * g_ref[...]).astype(BF16)
        return carry
    lax.fori_loop(0, tm // NORM_ROWS, body, 0)

    row = lax.broadcasted_iota(jnp.int32, (PERM_CHUNK, PERM_CHUNK), 0)
    col = lax.broadcasted_iota(jnp.int32, (PERM_CHUNK, PERM_CHUNK), 1)
    for d, o_ref in zip(dils[1:], out_refs[1:]):
        sub = tm // d
        csub = PERM_CHUNK // d
        perm = jnp.where((row % csub) * d + row // csub == col, 1.0, 0.0).astype(BF16)
        for c in range(tm // PERM_CHUNK):
            hp = jnp.dot(perm, nat_ref[c * PERM_CHUNK:(c + 1) * PERM_CHUNK, :],
                         preferred_element_type=F32).astype(BF16)
            for r in range(d):
                o_ref[r * sub + c * csub:r * sub + (c + 1) * csub, :] = hp[r * csub:(r + 1) * csub, :]


def _norm(x, g, dils=(1,)):
    b, t, k = x.shape
    tm = min(ROW_TILE, t)
    assert dils[0] == 1 and t % tm == 0 and tm % PERM_CHUNK == 0
    blk = pl.BlockSpec((None, tm, k), lambda bi, i: (bi, i, 0))
    return pl.pallas_call(
        functools.partial(_norm_kernel, dils=dils),
        grid=(b, t // tm),
        in_specs=[blk, pl.BlockSpec((1, k), lambda bi, i: (0, 0))],
        out_specs=[blk] * len(dils),
        out_shape=[jax.ShapeDtypeStruct((b, t, k), BF16)] * len(dils),
        compiler_params=_cparams("parallel", "parallel"),
        name="pre_norm",
    )(x, g.reshape(1, k))


def _proj_kernel(h_ref, w_ref, o_ref):
    res = jnp.dot(h_ref[...], w_ref[...], preferred_element_type=F32).astype(o_ref.dtype)
    o_ref[...] = res.reshape(o_ref.shape)


def _proj(h, w, *, tn, out_dtype, dil=1):
    b, t, k = h.shape
    n = w.shape[1]
    tm = min(ROW_TILE, t)
    tn = min(tn, n)
    assert t % tm == 0 and n % tn == 0 and tm % dil == 0
    return pl.pallas_call(
        _proj_kernel,
        grid=(b, t // tm, n // tn),
        in_specs=[pl.BlockSpec((None, tm, k), lambda bi, i, j: (bi, i, 0)),
                  pl.BlockSpec((k, tn), lambda bi, i, j: (0, j))],
        out_specs=pl.BlockSpec((None, dil, tm // dil, tn), lambda bi, i, j: (bi, 0, i, j)),
        out_shape=jax.ShapeDtypeStruct((b, dil, t // dil, n), out_dtype),
        compiler_params=_cparams("parallel", "parallel", "parallel"),
        name="proj",
    )(h, w)


def _out_proj_kernel(gt_ref, gm_ref, wt_ref, wm_ref, x_ref, o_ref):
    acc = jnp.dot(gt_ref[...], wt_ref[...], preferred_element_type=F32)
    acc = acc + jnp.dot(gm_ref[...], wm_ref[...], preferred_element_type=F32)
    o_ref[...] = x_ref[...] + acc


def _out_proj(gt, gm, wt, wm, x, *, tm, tn):
    m, n = x.shape
    tm = min(tm, m)
    assert m % tm == 0 and n % tn == 0
    return pl.pallas_call(
        _out_proj_kernel,
        grid=(m // tm, n // tn),
        in_specs=[pl.BlockSpec((tm, gt.shape[1]), lambda i, j: (i, 0)),
                  pl.BlockSpec((tm, gm.shape[1]), lambda i, j: (i, 0)),
                  pl.BlockSpec((wt.shape[0], tn), lambda i, j: (0, j)),
                  pl.BlockSpec((wm.shape[0], tn), lambda i, j: (0, j)),
                  pl.BlockSpec((tm, tn), lambda i, j: (i, j))],
        out_specs=pl.BlockSpec((tm, tn), lambda i, j: (i, j)),
        out_shape=jax.ShapeDtypeStruct((m, n), F32),
        compiler_params=_cparams("parallel", "arbitrary"),
        name="out_proj",
    )(gt, gm, wt, wm, x)


def _rmsnorm_kernel(x_ref, g_ref, o_ref):
    xf = x_ref[...]
    ms = jnp.mean(xf * xf, axis=-1, keepdims=True)
    o_ref[...] = (xf * lax.rsqrt(ms + EPS)) * g_ref[...]


def _rmsnorm(x, g, *, tm):
    m, k = x.shape
    tm = min(tm, m)
    return pl.pallas_call(
        _rmsnorm_kernel,
        grid=(m // tm,),
        in_specs=[pl.BlockSpec((tm, k), lambda i: (i, 0)),
                  pl.BlockSpec((1, k), lambda i: (0, 0))],
        out_specs=pl.BlockSpec((tm, k), lambda i: (i, 0)),
        out_shape=jax.ShapeDtypeStruct((m, k), F32),
        compiler_params=_cparams("parallel"),
        name="final_rmsnorm",
    )(x, g.reshape(1, k))


def _mem_attn_kernel(q_ref, z_ref, mk_ref, mv_ref, o_ref):
    scale = MEM_HEAD_DIM ** -0.5
    for h in range(MEM_HEADS):
        sl = slice(h * MEM_HEAD_DIM, (h + 1) * MEM_HEAD_DIM)
        s = lax.dot_general(q_ref[:, sl], mk_ref[:, sl], (((1,), (1,)), ((), ())),
                            preferred_element_type=F32) * scale
        m = jnp.max(s, axis=-1, keepdims=True)
        p = jnp.exp(s - m)
        den = jnp.sum(p, axis=-1, keepdims=True)
        o = jnp.dot(p.astype(BF16), mv_ref[:, sl], preferred_element_type=F32) / den
        z = z_ref[:, sl].astype(F32)
        o_ref[:, sl] = (o * _silu(z)).astype(o_ref.dtype)


def _mem_attn(proj, kv, *, q_block, z_block, tq):
    b, t, _ = proj.shape
    tq = min(tq, t)
    return pl.pallas_call(
        _mem_attn_kernel,
        grid=(b, t // tq),
        in_specs=[pl.BlockSpec((None, tq, MEM_WIDTH), lambda bi, i: (bi, i, q_block)),
                  pl.BlockSpec((None, tq, MEM_WIDTH), lambda bi, i: (bi, i, z_block)),
                  pl.BlockSpec((None, N_MEM, MEM_WIDTH), lambda bi, i: (bi, 0, 0)),
                  pl.BlockSpec((None, N_MEM, MEM_WIDTH), lambda bi, i: (bi, 0, 1))],
        out_specs=pl.BlockSpec((None, tq, MEM_WIDTH), lambda bi, i: (bi, i, 0)),
        out_shape=jax.ShapeDtypeStruct((b, t, MEM_WIDTH), BF16),
        compiler_params=_cparams("parallel", "parallel"),
        name="mem_attn",
    )(proj, proj, kv, kv)


def _ssd_kernel(xbc_ref, z_ref, dt_ref, cw_ref, cb_ref, dtb_ref, alog_ref, dskip_ref, ng_ref,
                o_ref, halo_ref, xc_ref, state_ref):
    q = SSD_CHUNK
    c = pl.program_id(1)

    @pl.when(c == 0)
    def _():
        halo_ref[...] = jnp.zeros_like(halo_ref)
        state_ref[...] = jnp.zeros_like(state_ref)

    for t in range(SSD_CONV_DIM // LANES):
        cs = slice(t * LANES, (t + 1) * LANES)
        u = xbc_ref[:, cs].astype(F32)
        ext = jnp.concatenate([halo_ref[:, cs], u], axis=0)
        acc = cb_ref[:, cs] + cw_ref[SSD_CONV - 1:SSD_CONV, cs] * u
        for k in range(SSD_CONV - 1):
            shifted = pltpu.roll(ext, SSD_CONV - 1 - k, axis=0)[HALO_ROWS:, :]
            acc = acc + cw_ref[k:k + 1, cs] * shifted
        xc_ref[:, cs] = _silu(acc)
        halo_ref[:, cs] = u[q - HALO_ROWS:, :]

    dtr = dt_ref[...] + dtb_ref[...]
    dt = jnp.maximum(dtr, 0.0) + jnp.log1p(jnp.exp(-jnp.abs(dtr)))
    a = -jnp.exp(alog_ref[...])
    dta = dt * a
    row = lax.broadcasted_iota(jnp.int32, (q, q), 0)
    col = lax.broadcasted_iota(jnp.int32, (q, q), 1)
    causal = row >= col
    tri = jnp.where(causal, 1.0, 0.0).astype(BF16)
    hi, mid, lo = _split3(dta)
    a_cs = (jnp.dot(tri, hi, preferred_element_type=F32)
            + jnp.dot(tri, mid, preferred_element_type=F32)
            + jnp.dot(tri, lo, preferred_element_type=F32))
    a_cs_t = a_cs.T
    dt_t = dt.T
    a_last = a_cs[q - 1:q, :]
    e_acs = jnp.exp(a_cs)
    w_state = jnp.exp(a_last - a_cs) * dt
    chunk_decay = jnp.exp(a_last)
    lane = lax.broadcasted_iota(jnp.int32, (q, LANES), 1)
    first_head = lane < SSD_HEAD_DIM
    lane_n = lax.broadcasted_iota(jnp.int32, (SSD_STATE, LANES), 1)
    first_head_n = lane_n < SSD_HEAD_DIM

    def pair_cols(arr, h0):
        n = arr.shape[0]
        c0 = jnp.broadcast_to(arr[:, h0:h0 + 1], (n, LANES))
        c1 = jnp.broadcast_to(arr[:, h0 + 1:h0 + 2], (n, LANES))
        return jnp.where(first_head[:n], c0, c1)

    for g in range(SSD_GROUPS):
        b_off = TOK_WIDTH + g * SSD_STATE
        c_off = TOK_WIDTH + SSD_BC_COLS + g * SSD_STATE
        bm = xc_ref[:, b_off:b_off + SSD_STATE]
        cm = xc_ref[:, c_off:c_off + SSD_STATE].astype(BF16)
        bm_t = bm.T.astype(BF16)
        cb = lax.dot_general(cm, bm.astype(BF16), (((1,), (1,)), ((), ())),
                             preferred_element_type=F32)
        sumsq = jnp.zeros((q, 1), F32)
        gated = []
        for pr in range(SSD_PAIRS_PER_GROUP):
            h0 = g * SSD_HEADS_PER_GROUP + 2 * pr
            xs = slice(g * SSD_GROUP_COLS + pr * LANES, g * SSD_GROUP_COLS + (pr + 1) * LANES)
            ss = slice(pr * LANES, (pr + 1) * LANES)
            xp = xc_ref[:, xs]
            xpb = xp.astype(BF16)
            res = []
            for h in (h0, h0 + 1):
                seg = a_cs[:, h:h + 1] - a_cs_t[h:h + 1, :]
                decay = jnp.exp(jnp.where(causal, seg, NEG))
                mat = (cb * decay * dt_t[h:h + 1, :]).astype(BF16)
                res.append(jnp.dot(mat, xpb, preferred_element_type=F32))
            y = jnp.where(first_head, res[0], res[1])
            s_prev = state_ref[g, :, ss]
            y = y + jnp.dot(cm, s_prev.astype(BF16), preferred_element_type=F32) * pair_cols(e_acs, h0)
            y = y + dskip_ref[:, xs] * xp
            xw = (xp * pair_cols(w_state, h0)).astype(BF16)
            cd = jnp.where(first_head_n,
                           jnp.broadcast_to(chunk_decay[:, h0:h0 + 1], (SSD_STATE, LANES)),
                           jnp.broadcast_to(chunk_decay[:, h0 + 1:h0 + 2], (SSD_STATE, LANES)))
            state_ref[g, :, ss] = cd * s_prev + jnp.dot(bm_t, xw, preferred_element_type=F32)
            z = z_ref[:, xs].astype(F32)
            gz = y * _silu(z)
            sumsq = sumsq + jnp.sum(gz * gz, axis=-1, keepdims=True)
            gated.append(gz)
        rinv = lax.rsqrt(sumsq * (1.0 / SSD_GROUP_COLS) + EPS)
        for pr in range(SSD_PAIRS_PER_GROUP):
            xs = slice(g * SSD_GROUP_COLS + pr * LANES, g * SSD_GROUP_COLS + (pr + 1) * LANES)
            o_ref[:, xs] = ((gated[pr] * rinv) * ng_ref[:, xs]).astype(o_ref.dtype)


def _ssd_mixer(proj, dt_raw, conv_w, conv_b, dt_bias, a_log, d_skip, ssd_norm_g):
    b, t, _ = proj.shape
    q = SSD_CHUNK
    pad = LANES - SSD_HEADS
    full = lambda bi, ci: (0, 0)
    return pl.pallas_call(
        _ssd_kernel,
        grid=(b, t // q),
        in_specs=[pl.BlockSpec((None, q, SSD_CONV_DIM), lambda bi, ci: (bi, ci, 0)),
                  pl.BlockSpec((None, q, TOK_WIDTH), lambda bi, ci: (bi, ci, 2)),
                  pl.BlockSpec((None, q, LANES), lambda bi, ci: (bi, ci, 0)),
                  pl.BlockSpec((SSD_CONV, SSD_CONV_DIM), full),
                  pl.BlockSpec((1, SSD_CONV_DIM), full),
                  pl.BlockSpec((1, LANES), full),
                  pl.BlockSpec((1, LANES), full),
                  pl.BlockSpec((1, TOK_WIDTH), full),
                  pl.BlockSpec((1, TOK_WIDTH), full)],
        out_specs=pl.BlockSpec((None, q, TOK_WIDTH), lambda bi, ci: (bi, ci, 0)),
        out_shape=jax.ShapeDtypeStruct((b, t, TOK_WIDTH), BF16),
        scratch_shapes=[pltpu.VMEM((HALO_ROWS, SSD_CONV_DIM), F32),
                        pltpu.VMEM((q, SSD_CONV_DIM), F32),
                        pltpu.VMEM((SSD_GROUPS, SSD_STATE, SSD_GROUP_COLS), F32)],
        compiler_params=_cparams("parallel", "arbitrary"),
        name="ssd_mixer",
    )(proj, proj, dt_raw, conv_w, conv_b.reshape(1, -1),
      jnp.pad(dt_bias, (0, pad)).reshape(1, LANES), jnp.pad(a_log, (0, pad)).reshape(1, LANES),
      jnp.repeat(d_skip, SSD_HEAD_DIM).reshape(1, TOK_WIDTH), ssd_norm_g.reshape(1, TOK_WIDTH))


ATTN_SLOTS = 4


def _dil_attn_kernel(slope_ref, q_ref, k_ref, v_ref, o_ref, lse_ref,
                     bias_ref, kt_ref, ktp_ref, vp_ref, s_ref, p_ref):
    c = ATTN_BLOCK
    n = pl.program_id(2)

    @pl.when((pl.program_id(0) == 0) & (pl.program_id(1) == 0) & (n == 0))
    def _():
        row = lax.broadcasted_iota(jnp.int32, (c, 2 * c), 0)
        col = lax.broadcasted_iota(jnp.int32, (c, 2 * c), 1)
        rel_pc = row + c - col
        rel_cp = jnp.where(col < c, row - col, row + 2 * c - col)
        band = lambda rel: (rel >= 0) & (rel <= c)
        for h in range(ATTN_HEADS):
            slope = slope_ref[h]
            bias_ref[0, h] = jnp.where(band(rel_pc), -slope * rel_pc.astype(F32), NEG)
            bias_ref[1, h] = jnp.where(band(rel_cp), -slope * rel_cp.astype(F32), NEG)
            bias_ref[2, h] = jnp.where(band(rel_cp) & (col < c), -slope * rel_cp.astype(F32), NEG)
        ktp_ref[...] = jnp.zeros_like(ktp_ref)
        vp_ref[...] = jnp.zeros_like(vp_ref)

    table_a = jnp.where(n == 0, 2, 1)
    lane = lax.broadcasted_iota(jnp.int32, (c, LANES), 1)
    lse_ref[...] = jnp.zeros_like(lse_ref)

    def head_cols(h):
        return slice(h * ATTN_HEAD_DIM, (h + 1) * ATTN_HEAD_DIM)

    def transpose_keys(blk, h):
        kt_ref[blk, head_cols(h), :] = k_ref[blk * c:(blk + 1) * c, head_cols(h)].T

    def scores(blk, h, slot):
        kt_a = kt_ref[0, head_cols(h), :]
        if blk == 0:
            kt_w = jnp.concatenate([kt_a, ktp_ref[head_cols(h), :]], axis=1)
            bias = bias_ref[table_a, h]
        else:
            kt_w = jnp.concatenate([kt_a, kt_ref[1, head_cols(h), :]], axis=1)
            bias = bias_ref[0, h]
        s = jnp.dot(q_ref[blk * c:(blk + 1) * c, head_cols(h)], kt_w, preferred_element_type=F32)
        s_ref[slot] = s + bias

    def softmax(blk, h, slot):
        s = s_ref[slot]
        m = jnp.max(s, axis=-1, keepdims=True)
        p = jnp.exp2(s - m)
        den = jnp.sum(p, axis=-1, keepdims=True)
        p_ref[slot] = p.astype(BF16)
        rows = slice(blk * c, (blk + 1) * c)
        lse_ref[rows, :] = jnp.where(lane == h, m, jnp.where(lane == ATTN_HEADS + h, den, lse_ref[rows, :]))

    def values(blk, h, slot):
        if blk == 0:
            vw = jnp.concatenate([v_ref[0:c, head_cols(h)], vp_ref[:, head_cols(h)]], axis=0)
        else:
            vw = v_ref[:, head_cols(h)]
        o_ref[blk * c:(blk + 1) * c, head_cols(h)] = jnp.dot(
            p_ref[slot], vw, preferred_element_type=F32).astype(o_ref.dtype)

    for blk in range(2):
        for h in range(ATTN_HEADS):
            transpose_keys(blk, h)

    items = [(blk, h) for blk in range(2) for h in range(ATTN_HEADS)]
    for i in range(len(items) + 2):
        if i < len(items):
            scores(*items[i], i % ATTN_SLOTS)
        if 1 <= i <= len(items):
            softmax(*items[i - 1], (i - 1) % ATTN_SLOTS)
        if i >= 2:
            values(*items[i - 2], (i - 2) % ATTN_SLOTS)

    ktp_ref[...] = kt_ref[1]
    vp_ref[...] = v_ref[c:2 * c, :]


def _dil_attn(qkv, slopes_d):
    b, d, n_sub, _ = qkv.shape
    c = ATTN_BLOCK
    assert n_sub % (2 * c) == 0
    blk = (None, None, 2 * c, TOK_WIDTH)
    col_block = lambda off: (lambda bi, r, n, *_: (bi, r, n, off))
    return pl.pallas_call(
        _dil_attn_kernel,
        grid_spec=pltpu.PrefetchScalarGridSpec(
            num_scalar_prefetch=1,
            grid=(b, d, n_sub // (2 * c)),
            in_specs=[pl.BlockSpec(blk, col_block(0)), pl.BlockSpec(blk, col_block(1)),
                      pl.BlockSpec(blk, col_block(2))],
            out_specs=[pl.BlockSpec(blk, col_block(0)),
                       pl.BlockSpec((None, None, 2 * c, LANES), col_block(0))],
            scratch_shapes=[pltpu.VMEM((3, ATTN_HEADS, c, 2 * c), F32),
                            pltpu.VMEM((2, TOK_WIDTH, c), BF16),
                            pltpu.VMEM((TOK_WIDTH, c), BF16),
                            pltpu.VMEM((c, TOK_WIDTH), BF16),
                            pltpu.VMEM((ATTN_SLOTS, c, 2 * c), F32),
                            pltpu.VMEM((ATTN_SLOTS, c, 2 * c), BF16)]),
        out_shape=[jax.ShapeDtypeStruct((b, d, n_sub, TOK_WIDTH), BF16),
                   jax.ShapeDtypeStruct((b, d, n_sub, LANES), F32)],
        compiler_params=_cparams("arbitrary", "arbitrary", "arbitrary"),
        name="dil_attn",
    )(slopes_d, qkv, qkv, qkv)


def _attn_mix_kernel(o0_ref, o1_ref, o2_ref, l0_ref, l1_ref, l2_ref, z_ref, out_ref, *, dils):
    tm = out_ref.shape[0]
    row = lax.broadcasted_iota(jnp.int32, (tm, tm), 0)
    col = lax.broadcasted_iota(jnp.int32, (tm, tm), 1)

    def perm(d):
        sub = tm // d
        return jnp.where((row % d) * sub + row // d == col, 1.0, 0.0).astype(BF16)

    def stacked(ref, d):
        return ref[0] if d == 1 else jnp.concatenate([ref[r] for r in range(d)], axis=0)

    perms = [None if d == 1 else perm(d) for d in dils]
    o_refs = (o0_ref, o1_ref, o2_ref)
    maxes, dens = [], []
    for ref, d, pm in zip((l0_ref, l1_ref, l2_ref), dils, perms):
        st = stacked(ref, d)
        if pm is not None:
            st = sum(jnp.dot(pm, part, preferred_element_type=F32) for part in _split3(st))
        maxes.append(st)
        dens.append(pltpu.roll(st, LANES - ATTN_HEADS, axis=1))
    mx = jnp.maximum(jnp.maximum(maxes[0], maxes[1]), maxes[2])
    es = [jnp.exp2(m - mx) for m in maxes]
    inv = 1.0 / (es[0] * dens[0] + es[1] * dens[1] + es[2] * dens[2])
    ws = [e * inv for e in es]
    for h in range(ATTN_HEADS):
        sl = slice(h * ATTN_HEAD_DIM, (h + 1) * ATTN_HEAD_DIM)
        y = None
        for ref, d, pm, w in zip(o_refs, dils, perms, ws):
            if d == 1:
                o = ref[0, :, sl].astype(F32)
            else:
                o = jnp.dot(pm, jnp.concatenate([ref[r, :, sl] for r in range(d)], axis=0),
                            preferred_element_type=F32)
            term = jnp.broadcast_to(w[:, h:h + 1], (tm, ATTN_HEAD_DIM)) * o
            y = term if y is None else y + term
        out_ref[:, sl] = (y * _silu(z_ref[:, sl].astype(F32))).astype(out_ref.dtype)


def _attn_mix(os_, ls_, rest, *, tm):
    b, t, _ = rest.shape
    tm = min(tm, t)
    dils = tuple(o.shape[1] for o in os_)
    grp = lambda d, w: pl.BlockSpec((None, d, tm // d, w), lambda bi, i: (bi, 0, i, 0))
    tok = pl.BlockSpec((None, tm, TOK_WIDTH), lambda bi, i: (bi, i, 0))
    return pl.pallas_call(
        functools.partial(_attn_mix_kernel, dils=dils),
        grid=(b, t // tm),
        in_specs=[grp(d, TOK_WIDTH) for d in dils] + [grp(d, LANES) for d in dils] + [tok],
        out_specs=tok,
        out_shape=jax.ShapeDtypeStruct((b, t, TOK_WIDTH), BF16),
        compiler_params=_cparams("parallel", "parallel"),
        name="attn_mix",
    )(*os_, *ls_, rest)


def _mem_kv(mem, mem_norm_g, w_mem_kv):
    b, n, dm = mem.shape
    (hm,) = _norm(mem.reshape(1, b * n, dm), mem_norm_g)
    kv = _proj(hm, w_mem_kv.astype(BF16), tn=1024, out_dtype=BF16)
    return kv.reshape(b, n, 2 * MEM_WIDTH)


def _finish_layer(x2, gated_tok, gated_mem, w_out):
    m = x2.shape[0]
    wt = w_out[:TOK_WIDTH].astype(BF16)
    wm = w_out[TOK_WIDTH:].astype(BF16)
    return _out_proj(gated_tok.reshape(m, TOK_WIDTH), gated_mem.reshape(m, MEM_WIDTH), wt, wm, x2,
                     tm=1024, tn=512)


def _ssd_layer(x2, bt, kv, norm_g, w_in, conv_w, conv_b, dt_bias, a_log, d_skip, ssd_norm_g, w_out):
    b, t = bt
    o_dt = SSD_CONV_DIM
    o_q = o_dt + SSD_HEADS
    o_z = o_q + MEM_WIDTH
    w_main = jnp.concatenate([w_in[:, :o_dt], w_in[:, o_q:]], axis=1).astype(BF16)
    w_dt = jnp.pad(w_in[:, o_dt:o_q], ((0, 0), (0, LANES - SSD_HEADS))).astype(BF16)
    x3 = x2.reshape(b, t, -1)
    (h,) = _norm(x3, norm_g)
    proj = _proj(h, w_main, tn=2048, out_dtype=BF16).reshape(b, t, -1)
    dt_raw = _proj(h, w_dt, tn=LANES, out_dtype=F32).reshape(b, t, LANES)
    gated_tok = _ssd_mixer(proj, dt_raw, conv_w, conv_b, dt_bias, a_log, d_skip, ssd_norm_g)
    gated_mem = _mem_attn(proj, kv, q_block=SSD_CONV_DIM // MEM_WIDTH,
                          z_block=(SSD_CONV_DIM + MEM_WIDTH + TOK_WIDTH) // MEM_WIDTH, tq=512)
    return _finish_layer(x2, gated_tok, gated_mem, w_out)


def _attn_layer(x2, bt, kv, norm_g, w_in, w_out):
    b, t = bt
    o_q = N_DIL * ATTN_GROUP_COLS
    o_z = o_q + MEM_WIDTH
    x3 = x2.reshape(b, t, -1)
    w_rest = jnp.concatenate([w_in[:, o_z:o_z + TOK_WIDTH], w_in[:, o_q:o_z], w_in[:, o_z + TOK_WIDTH:]],
                             axis=1).astype(BF16)
    dils = tuple(d for _, d in DILATED_GROUPS)
    hs = dict(zip(dils, _norm(x3, norm_g, dils)))
    rest = _proj(hs[1], w_rest, tn=2560, out_dtype=BF16).reshape(b, t, -1)
    slopes = jnp.exp2(-ALIBI_MAX_EXP * jnp.arange(1, N_DIL * ATTN_HEADS + 1, dtype=F32) / (N_DIL * ATTN_HEADS))
    slopes = slopes.reshape(N_DIL, ATTN_HEADS)
    q_scale = (ATTN_HEAD_DIM ** -0.5) * LOG2E
    outs, lses = [], []
    for g, (window, d) in enumerate(DILATED_GROUPS):
        assert window // d == ATTN_BLOCK
        w_g = w_in[:, g * ATTN_GROUP_COLS:(g + 1) * ATTN_GROUP_COLS]
        w_g = jnp.concatenate([w_g[:, :TOK_WIDTH] * q_scale, w_g[:, TOK_WIDTH:]], axis=1).astype(BF16)
        qkv = _proj(hs[d], w_g, tn=1536, out_dtype=BF16, dil=d)
        o, lse = _dil_attn(qkv, slopes[g] * (float(d) * LOG2E))
        outs.append(o)
        lses.append(lse)
    gated_tok = _attn_mix(outs, lses, rest, tm=256)
    gated_mem = _mem_attn(rest, kv, q_block=TOK_WIDTH // MEM_WIDTH,
                          z_block=(TOK_WIDTH + MEM_WIDTH) // MEM_WIDTH, tq=512)
    return _finish_layer(x2, gated_tok, gated_mem, w_out)


def kernel(x, mem, mem_norm_g, final_norm_g, norm_g_0, w_in_0, conv_w_0, conv_b_0, dt_bias_0, a_log_0, d_skip_0, ssd_norm_g_0, w_mem_kv_0, w_out_0, norm_g_1, w_in_1, w_mem_kv_1, w_out_1, norm_g_2, w_in_2, conv_w_2, conv_b_2, dt_bias_2, a_log_2, d_skip_2, ssd_norm_g_2, w_mem_kv_2, w_out_2, norm_g_3, w_in_3, w_mem_kv_3, w_out_3):
    b, t, dm = x.shape
    bt = (b, t)
    x2 = x.reshape(b * t, dm)
    x2 = _ssd_layer(x2, bt, _mem_kv(mem, mem_norm_g, w_mem_kv_0), norm_g_0, w_in_0, conv_w_0, conv_b_0,
                    dt_bias_0, a_log_0, d_skip_0, ssd_norm_g_0, w_out_0)
    x2 = _attn_layer(x2, bt, _mem_kv(mem, mem_norm_g, w_mem_kv_1), norm_g_1, w_in_1, w_out_1)
    x2 = _ssd_layer(x2, bt, _mem_kv(mem, mem_norm_g, w_mem_kv_2), norm_g_2, w_in_2, conv_w_2, conv_b_2,
                    dt_bias_2, a_log_2, d_skip_2, ssd_norm_g_2, w_out_2)
    x2 = _attn_layer(x2, bt, _mem_kv(mem, mem_norm_g, w_mem_kv_3), norm_g_3, w_in_3, w_out_3)
    return _rmsnorm(x2, final_norm_g, tm=512).reshape(b, t, dm)
```

```python
import functools

import jax
import jax.numpy as jnp
from jax import lax
from jax.experimental import pallas as pl
from jax.experimental.pallas import tpu as pltpu

F32 = jnp.float32
BF16 = jnp.bfloat16
EPS = 1e-6
NEG = -1e30
LOG2E = 1.4426950408889634

LANES = 128
D_MODEL = 2048
N_MEM = 256
MIX_WIDTH = 2 * D_MODEL
MEM_WIDTH = MIX_WIDTH // 4
TOK_WIDTH = MIX_WIDTH - MEM_WIDTH
MEM_HEADS = 4
MEM_HEAD_DIM = MEM_WIDTH // MEM_HEADS

SSD_HEAD_DIM = 64
SSD_HEADS = TOK_WIDTH // SSD_HEAD_DIM
SSD_GROUPS = 8
SSD_HEADS_PER_GROUP = SSD_HEADS // SSD_GROUPS
SSD_GROUP_COLS = SSD_HEADS_PER_GROUP * SSD_HEAD_DIM
SSD_PAIRS_PER_GROUP = SSD_GROUP_COLS // LANES
SSD_STATE = 128
SSD_CONV = 4
SSD_CHUNK = 128
SSD_BC_COLS = SSD_GROUPS * SSD_STATE
SSD_CONV_DIM = TOK_WIDTH + 2 * SSD_BC_COLS
HALO_ROWS = 8

ATTN_HEAD_DIM = 128
ATTN_HEADS = TOK_WIDTH // ATTN_HEAD_DIM
DILATED_GROUPS = ((128, 1), (512, 4), (2048, 16))
N_DIL = len(DILATED_GROUPS)
ATTN_GROUP_COLS = 3 * TOK_WIDTH
ATTN_BLOCK = 128
ALIBI_MAX_EXP = 8.0

VMEM_LIMIT = 56 * 1024 * 1024


def _cparams(*sem):
    return pltpu.CompilerParams(dimension_semantics=sem, vmem_limit_bytes=VMEM_LIMIT)


def _silu(v):
    hv = 0.5 * v
    return hv + hv * jnp.tanh(hv)


def _split3(v):
    hi = v.astype(BF16)
    r1 = v - hi.astype(F32)
    mid = r1.astype(BF16)
    lo = (r1 - mid.astype(F32)).astype(BF16)
    return hi, mid, lo


ROW_TILE = 1024
PERM_CHUNK = 256
NORM_ROWS = 32


def _norm_kernel(x_ref, g_ref, *out_refs, dils):
    tm = x_ref.shape[0]
    nat_ref = out_refs[0]

    def body(i, carry):
        r = pl.multiple_of(i * NORM_ROWS, NORM_ROWS)
        xf = x_ref[pl.ds(r, NORM_ROWS), :]
        ms = jnp.mean(xf * xf, axis=-1, keepdims=True)
        nat_ref[pl.ds(r, NORM_ROWS), :] = ((xf * lax.rsqrt(ms + EPS)) * g_ref[...]).astype(BF16)
        return carry
    lax.fori_loop(0, tm // NORM_ROWS, body, 0)

    row = lax.broadcasted_iota(jnp.int32, (PERM_CHUNK, PERM_CHUNK), 0)
    col = lax.broadcasted_iota(jnp.int32, (PERM_CHUNK, PERM_CHUNK), 1)
    for d, o_ref in zip(dils[1:], out_refs[1:]):
        sub = tm // d
        csub = PERM_CHUNK // d
        perm = jnp.where((row % csub) * d + row // csub == col, 1.0, 0.0).astype(BF16)
        for c in range(tm // PERM_CHUNK):
            hp = jnp.dot(perm, nat_ref[c * PERM_CHUNK:(c + 1) * PERM_CHUNK, :],
                         preferred_element_type=F32).astype(BF16)
            for r in range(d):
                o_ref[r * sub + c * csub:r * sub + (c + 1) * csub, :] = hp[r * csub:(r + 1) * csub, :]


def _norm(x, g, dils=(1,)):
    b, t, k = x.shape
    tm = min(ROW_TILE, t)
    assert dils[0] == 1 and t % tm == 0 and tm % PERM_CHUNK == 0
    blk = pl.BlockSpec((None, tm, k), lambda bi, i: (bi, i, 0))
    return pl.pallas_call(
        functools.partial(_norm_kernel, dils=dils),
        grid=(b, t // tm),
        in_specs=[blk, pl.BlockSpec((1, k), lambda bi, i: (0, 0))],
        out_specs=[blk] * len(dils),
        out_shape=[jax.ShapeDtypeStruct((b, t, k), BF16)] * len(dils),
        compiler_params=_cparams("parallel", "parallel"),
        name="pre_norm",
    )(x, g.reshape(1, k))


def _proj_kernel(h_ref, w_ref, o_ref):
    res = jnp.dot(h_ref[...], w_ref[...], preferred_element_type=F32).astype(o_ref.dtype)
    o_ref[...] = res.reshape(o_ref.shape)


def _proj(h, w, *, tn, out_dtype, dil=1, col0=0, ncols=None):
    b, t, k = h.shape
    n = w.shape[1] if ncols is None else ncols
    tm = min(ROW_TILE, t)
    tn = min(tn, n)
    assert t % tm == 0 and n % tn == 0 and col0 % tn == 0 and tm % dil == 0
    j0 = col0 // tn
    return pl.pallas_call(
        _proj_kernel,
        grid=(b, t // tm, n // tn),
        in_specs=[pl.BlockSpec((None, tm, k), lambda bi, i, j: (bi, i, 0)),
                  pl.BlockSpec((k, tn), lambda bi, i, j: (0, j0 + j))],
        out_specs=pl.BlockSpec((None, dil, tm // dil, tn), lambda bi, i, j: (bi, 0, i, j)),
        out_shape=jax.ShapeDtypeStruct((b, dil, t // dil, n), out_dtype),
        compiler_params=_cparams("parallel", "parallel", "parallel"),
        name="proj",
    )(h, w)


OUT_ROWS = 512


def _out_proj_kernel(gt_ref, gm_ref, wt_ref, wm_ref, x_ref, *rest, mode):
    o_ref = rest[-1] if mode != "norm" else rest[-2]
    acc = jnp.dot(gt_ref[...], wt_ref[...], preferred_element_type=F32)
    acc = acc + jnp.dot(gm_ref[...], wm_ref[...], preferred_element_type=F32)
    o_ref[...] = x_ref[...] + acc
    if mode == "plain":
        return
    g_ref = rest[0]

    def body(i, carry):
        r = pl.multiple_of(i * NORM_ROWS, NORM_ROWS)
        xf = o_ref[pl.ds(r, NORM_ROWS), :]
        ms = jnp.mean(xf * xf, axis=-1, keepdims=True)
        hn = (xf * lax.rsqrt(ms + EPS)) * g_ref[...]
        if mode == "norm":
            rest[-1][pl.ds(r, NORM_ROWS), :] = hn.astype(BF16)
        else:
            o_ref[pl.ds(r, NORM_ROWS), :] = hn
        return carry
    lax.fori_loop(0, o_ref.shape[0] // NORM_ROWS, body, 0)


def _out_proj(gt, gm, wt, wm, x, g=None, *, mode):
    m, n = x.shape
    tm = min(OUT_ROWS, m)
    assert m % tm == 0 and tm % NORM_ROWS == 0 and (g is None) == (mode == "plain")
    rows = lambda width: pl.BlockSpec((tm, width), lambda i: (i, 0))
    resident = lambda arr: pl.BlockSpec(arr.shape, lambda i: (0, 0), pipeline_mode=pl.Buffered(1))
    in_specs = [rows(gt.shape[1]), rows(gm.shape[1]), resident(wt), resident(wm), rows(n)]
    args = [gt, gm, wt, wm, x]
    if g is not None:
        in_specs.append(pl.BlockSpec((1, n), lambda i: (0, 0)))
        args.append(g.reshape(1, n))
    out_shape = [jax.ShapeDtypeStruct((m, n), F32)]
    if mode == "norm":
        out_shape.append(jax.ShapeDtypeStruct((m, n), BF16))
    return pl.pallas_call(
        functools.partial(_out_proj_kernel, mode=mode),
        grid=(m // tm,),
        in_specs=in_specs,
        out_specs=[rows(n)] * len(out_shape),
        out_shape=out_shape,
        compiler_params=_cparams("parallel"),
        name="out_proj",
    )(*args)


def _mem_attn_kernel(q_ref, z_ref, mk_ref, mv_ref, o_ref):
    scale = MEM_HEAD_DIM ** -0.5
    for h in range(MEM_HEADS):
        sl = slice(h * MEM_HEAD_DIM, (h + 1) * MEM_HEAD_DIM)
        s = lax.dot_general(q_ref[:, sl], mk_ref[:, sl], (((1,), (1,)), ((), ())),
                            preferred_element_type=F32) * scale
        m = jnp.max(s, axis=-1, keepdims=True)
        p = jnp.exp(s - m)
        den = jnp.sum(p, axis=-1, keepdims=True)
        o = jnp.dot(p.astype(BF16), mv_ref[:, sl], preferred_element_type=F32) / den
        z = z_ref[:, sl].astype(F32)
        o_ref[:, sl] = (o * _silu(z)).astype(o_ref.dtype)


def _mem_attn(proj, kv, *, q_block, z_block, tq):
    b, t, _ = proj.shape
    tq = min(tq, t)
    return pl.pallas_call(
        _mem_attn_kernel,
        grid=(b, t // tq),
        in_specs=[pl.BlockSpec((None, tq, MEM_WIDTH), lambda bi, i: (bi, i, q_block)),
                  pl.BlockSpec((None, tq, MEM_WIDTH), lambda bi, i: (bi, i, z_block)),
                  pl.BlockSpec((None, N_MEM, MEM_WIDTH), lambda bi, i: (bi, 0, 0)),
                  pl.BlockSpec((None, N_MEM, MEM_WIDTH), lambda bi, i: (bi, 0, 1))],
        out_specs=pl.BlockSpec((None, tq, MEM_WIDTH), lambda bi, i: (bi, i, 0)),
        out_shape=jax.ShapeDtypeStruct((b, t, MEM_WIDTH), BF16),
        compiler_params=_cparams("parallel", "parallel"),
        name="mem_attn",
    )(proj, proj, kv, kv)


def _ssd_kernel(xbc_ref, z_ref, dt_ref, cw_ref, cb_ref, dtb_ref, alog_ref, dskip_ref, ng_ref,
                o_ref, halo_ref, xc_ref, state_ref):
    q = SSD_CHUNK
    c = pl.program_id(1)

    @pl.when(c == 0)
    def _():
        halo_ref[...] = jnp.zeros_like(halo_ref)
        state_ref[...] = jnp.zeros_like(state_ref)

    for t in range(SSD_CONV_DIM // LANES):
        cs = slice(t * LANES, (t + 1) * LANES)
        u = xbc_ref[:, cs].astype(F32)
        ext = jnp.concatenate([halo_ref[:, cs], u], axis=0)
        acc = cb_ref[:, cs] + cw_ref[SSD_CONV - 1:SSD_CONV, cs] * u
        for k in range(SSD_CONV - 1):
            shifted = pltpu.roll(ext, SSD_CONV - 1 - k, axis=0)[HALO_ROWS:, :]
            acc = acc + cw_ref[k:k + 1, cs] * shifted
        xc_ref[:, cs] = _silu(acc)
        halo_ref[:, cs] = u[q - HALO_ROWS:, :]

    dtr = dt_ref[...] + dtb_ref[...]
    dt = jnp.maximum(dtr, 0.0) + jnp.log1p(jnp.exp(-jnp.abs(dtr)))
    a = -jnp.exp(alog_ref[...])
    dta = dt * a
    row = lax.broadcasted_iota(jnp.int32, (q, q), 0)
    col = lax.broadcasted_iota(jnp.int32, (q, q), 1)
    causal = row >= col
    tri = jnp.where(causal, 1.0, 0.0).astype(BF16)
    hi, mid, lo = _split3(dta)
    a_cs = (jnp.dot(tri, hi, preferred_element_type=F32)
            + jnp.dot(tri, mid, preferred_element_type=F32)
            + jnp.dot(tri, lo, preferred_element_type=F32))
    a_cs_t = a_cs.T
    dt_t = dt.T
    a_last = a_cs[q - 1:q, :]
    e_acs = jnp.exp(a_cs)
    w_state = jnp.exp(a_last - a_cs) * dt
    chunk_decay = jnp.exp(a_last)
    lane = lax.broadcasted_iota(jnp.int32, (q, LANES), 1)
    first_head = lane < SSD_HEAD_DIM
    lane_n = lax.broadcasted_iota(jnp.int32, (SSD_STATE, LANES), 1)
    first_head_n = lane_n < SSD_HEAD_DIM

    def pair_cols(arr, h0):
        n = arr.shape[0]
        c0 = jnp.broadcast_to(arr[:, h0:h0 + 1], (n, LANES))
        c1 = jnp.broadcast_to(arr[:, h0 + 1:h0 + 2], (n, LANES))
        return jnp.where(first_head[:n], c0, c1)

    for g in range(SSD_GROUPS):
        b_off = TOK_WIDTH + g * SSD_STATE
        c_off = TOK_WIDTH + SSD_BC_COLS + g * SSD_STATE
        bm = xc_ref[:, b_off:b_off + SSD_STATE]
        cm = xc_ref[:, c_off:c_off + SSD_STATE].astype(BF16)
        bm_t = bm.T.astype(BF16)
        cb = lax.dot_general(cm, bm.astype(BF16), (((1,), (1,)), ((), ())),
                             preferred_element_type=F32)
        sumsq = jnp.zeros((q, 1), F32)
        gated = []
        for pr in range(SSD_PAIRS_PER_GROUP):
            h0 = g * SSD_HEADS_PER_GROUP + 2 * pr
            xs = slice(g * SSD_GROUP_COLS + pr * LANES, g * SSD_GROUP_COLS + (pr + 1) * LANES)
            ss = slice(pr * LANES, (pr + 1) * LANES)
            xp = xc_ref[:, xs]
            xpb = xp.astype(BF16)
            res = []
            for h in (h0, h0 + 1):
                seg = a_cs[:, h:h + 1] - a_cs_t[h:h + 1, :]
                decay = jnp.exp(jnp.where(causal, seg, NEG))
                mat = (cb * decay * dt_t[h:h + 1, :]).astype(BF16)
                res.append(jnp.dot(mat, xpb, preferred_element_type=F32))
            y = jnp.where(first_head, res[0], res[1])
            s_prev = state_ref[g, :, ss]
            y = y + jnp.dot(cm, s_prev.astype(BF16), preferred_element_type=F32) * pair_cols(e_acs, h0)
            y = y + dskip_ref[:, xs] * xp
            xw = (xp * pair_cols(w_state, h0)).astype(BF16)
            cd = jnp.where(first_head_n,
                           jnp.broadcast_to(chunk_decay[:, h0:h0 + 1], (SSD_STATE, LANES)),
                           jnp.broadcast_to(chunk_decay[:, h0 + 1:h0 + 2], (SSD_STATE, LANES)))
            state_ref[g, :, ss] = cd * s_prev + jnp.dot(bm_t, xw, preferred_element_type=F32)
            z = z_ref[:, xs].astype(F32)
            gz = y * _silu(z)
            sumsq = sumsq + jnp.sum(gz * gz, axis=-1, keepdims=True)
            gated.append(gz)
        rinv = lax.rsqrt(sumsq * (1.0 / SSD_GROUP_COLS) + EPS)
        for pr in range(SSD_PAIRS_PER_GROUP):
            xs = slice(g * SSD_GROUP_COLS + pr * LANES, g * SSD_GROUP_COLS + (pr + 1) * LANES)
            o_ref[:, xs] = ((gated[pr] * rinv) * ng_ref[:, xs]).astype(o_ref.dtype)


def _ssd_mixer(proj, dt_raw, conv_w, conv_b, dt_bias, a_log, d_skip, ssd_norm_g):
    b, t, _ = proj.shape
    q = SSD_CHUNK
    pad = LANES - SSD_HEADS
    full = lambda bi, ci: (0, 0)
    return pl.pallas_call(
        _ssd_kernel,
        grid=(b, t // q),
        in_specs=[pl.BlockSpec((None, q, SSD_CONV_DIM), lambda bi, ci: (bi, ci, 0)),
                  pl.BlockSpec((None, q, TOK_WIDTH), lambda bi, ci: (bi, ci, 2)),
                  pl.BlockSpec((None, q, LANES), lambda bi, ci: (bi, ci, 0)),
                  pl.BlockSpec((SSD_CONV, SSD_CONV_DIM), full),
                  pl.BlockSpec((1, SSD_CONV_DIM), full),
                  pl.BlockSpec((1, LANES), full),
                  pl.BlockSpec((1, LANES), full),
                  pl.BlockSpec((1, TOK_WIDTH), full),
                  pl.BlockSpec((1, TOK_WIDTH), full)],
        out_specs=pl.BlockSpec((None, q, TOK_WIDTH), lambda bi, ci: (bi, ci, 0)),
        out_shape=jax.ShapeDtypeStruct((b, t, TOK_WIDTH), BF16),
        scratch_shapes=[pltpu.VMEM((HALO_ROWS, SSD_CONV_DIM), F32),
                        pltpu.VMEM((q, SSD_CONV_DIM), F32),
                        pltpu.VMEM((SSD_GROUPS, SSD_STATE, SSD_GROUP_COLS), F32)],
        compiler_params=_cparams("parallel", "arbitrary"),
        name="ssd_mixer",
    )(proj, proj, dt_raw, conv_w, conv_b.reshape(1, -1),
      jnp.pad(dt_bias, (0, pad)).reshape(1, LANES), jnp.pad(a_log, (0, pad)).reshape(1, LANES),
      jnp.repeat(d_skip, SSD_HEAD_DIM).reshape(1, TOK_WIDTH), ssd_norm_g.reshape(1, TOK_WIDTH))


ATTN_SLOTS = 4


def _dil_attn_kernel(slope_ref, q_ref, k_ref, v_ref, o_ref, lse_ref,
                     bias_ref, kt_ref, ktp_ref, vp_ref, s_ref, p_ref):
    c = ATTN_BLOCK
    n = pl.program_id(2)

    @pl.when((pl.program_id(0) == 0) & (pl.program_id(1) == 0) & (n == 0))
    def _():
        row = lax.broadcasted_iota(jnp.int32, (c, 2 * c), 0)
        col = lax.broadcasted_iota(jnp.int32, (c, 2 * c), 1)
        rel_pc = row + c - col
        rel_cp = jnp.where(col < c, row - col, row + 2 * c - col)
        band = lambda rel: (rel >= 0) & (rel <= c)
        for h in range(ATTN_HEADS):
            slope = slope_ref[h]
            bias_ref[0, h] = jnp.where(band(rel_pc), -slope * rel_pc.astype(F32), NEG)
            bias_ref[1, h] = jnp.where(band(rel_cp), -slope * rel_cp.astype(F32), NEG)
            bias_ref[2, h] = jnp.where(band(rel_cp) & (col < c), -slope * rel_cp.astype(F32), NEG)
        ktp_ref[...] = jnp.zeros_like(ktp_ref)
        vp_ref[...] = jnp.zeros_like(vp_ref)

    table_a = jnp.where(n == 0, 2, 1)
    lane = lax.broadcasted_iota(jnp.int32, (c, LANES), 1)
    lse_ref[...] = jnp.zeros_like(lse_ref)

    def head_cols(h):
        return slice(h * ATTN_HEAD_DIM, (h + 1) * ATTN_HEAD_DIM)

    def transpose_keys(blk, h):
        kt_ref[blk, head_cols(h), :] = k_ref[blk * c:(blk + 1) * c, head_cols(h)].T

    def scores(blk, h, slot):
        kt_a = kt_ref[0, head_cols(h), :]
        if blk == 0:
            kt_w = jnp.concatenate([kt_a, ktp_ref[head_cols(h), :]], axis=1)
            bias = bias_ref[table_a, h]
        else:
            kt_w = jnp.concatenate([kt_a, kt_ref[1, head_cols(h), :]], axis=1)
            bias = bias_ref[0, h]
        s = jnp.dot(q_ref[blk * c:(blk + 1) * c, head_cols(h)], kt_w, preferred_element_type=F32)
        s_ref[slot] = s + bias

    def softmax(blk, h, slot):
        s = s_ref[slot]
        m = jnp.max(s, axis=-1, keepdims=True)
        p = jnp.exp2(s - m)
        den = jnp.sum(p, axis=-1, keepdims=True)
        p_ref[slot] = p.astype(BF16)
        rows = slice(blk * c, (blk + 1) * c)
        lse_ref[rows, :] = jnp.where(lane == h, m, jnp.where(lane == ATTN_HEADS + h, den, lse_ref[rows, :]))

    def values(blk, h, slot):
        if blk == 0:
            vw = jnp.concatenate([v_ref[0:c, head_cols(h)], vp_ref[:, head_cols(h)]], axis=0)
        else:
            vw = v_ref[:, head_cols(h)]
        o_ref[blk * c:(blk + 1) * c, head_cols(h)] = jnp.dot(
            p_ref[slot], vw, preferred_element_type=F32).astype(o_ref.dtype)

    for blk in range(2):
        for h in range(ATTN_HEADS):
            transpose_keys(blk, h)

    items = [(blk, h) for blk in range(2) for h in range(ATTN_HEADS)]
    for i in range(len(items) + 2):
        if i < len(items):
            scores(*items[i], i % ATTN_SLOTS)
        if 1 <= i <= len(items):
            softmax(*items[i - 1], (i - 1) % ATTN_SLOTS)
        if i >= 2:
            values(*items[i - 2], (i - 2) % ATTN_SLOTS)

    ktp_ref[...] = kt_ref[1]
    vp_ref[...] = v_ref[c:2 * c, :]


def _dil_attn(qkv, slopes_d):
    b, d, n_sub, _ = qkv.shape
    c = ATTN_BLOCK
    assert n_sub % (2 * c) == 0
    blk = (None, None, 2 * c, TOK_WIDTH)
    col_block = lambda off: (lambda bi, r, n, *_: (bi, r, n, off))
    return pl.pallas_call(
        _dil_attn_kernel,
        grid_spec=pltpu.PrefetchScalarGridSpec(
            num_scalar_prefetch=1,
            grid=(b, d, n_sub // (2 * c)),
            in_specs=[pl.BlockSpec(blk, col_block(0)), pl.BlockSpec(blk, col_block(1)),
                      pl.BlockSpec(blk, col_block(2))],
            out_specs=[pl.BlockSpec(blk, col_block(0)),
                       pl.BlockSpec((None, None, 2 * c, LANES), col_block(0))],
            scratch_shapes=[pltpu.VMEM((3, ATTN_HEADS, c, 2 * c), F32),
                            pltpu.VMEM((2, TOK_WIDTH, c), BF16),
                            pltpu.VMEM((TOK_WIDTH, c), BF16),
                            pltpu.VMEM((c, TOK_WIDTH), BF16),
                            pltpu.VMEM((ATTN_SLOTS, c, 2 * c), F32),
                            pltpu.VMEM((ATTN_SLOTS, c, 2 * c), BF16)]),
        out_shape=[jax.ShapeDtypeStruct((b, d, n_sub, TOK_WIDTH), BF16),
                   jax.ShapeDtypeStruct((b, d, n_sub, LANES), F32)],
        compiler_params=_cparams("arbitrary", "arbitrary", "arbitrary"),
        name="dil_attn",
    )(slopes_d, qkv, qkv, qkv)


def _attn_mix_kernel(o0_ref, o1_ref, o2_ref, l0_ref, l1_ref, l2_ref, z_ref, out_ref, *, dils):
    tm = out_ref.shape[0]
    row = lax.broadcasted_iota(jnp.int32, (tm, tm), 0)
    col = lax.broadcasted_iota(jnp.int32, (tm, tm), 1)

    def perm(d):
        sub = tm // d
        return jnp.where((row % d) * sub + row // d == col, 1.0, 0.0).astype(BF16)

    def stacked(ref, d):
        return ref[0] if d == 1 else jnp.concatenate([ref[r] for r in range(d)], axis=0)

    perms = [None if d == 1 else perm(d) for d in dils]
    o_refs = (o0_ref, o1_ref, o2_ref)
    maxes, dens = [], []
    for ref, d, pm in zip((l0_ref, l1_ref, l2_ref), dils, perms):
        st = stacked(ref, d)
        if pm is not None:
            st = sum(jnp.dot(pm, part, preferred_element_type=F32) for part in _split3(st))
        maxes.append(st)
        dens.append(pltpu.roll(st, LANES - ATTN_HEADS, axis=1))
    mx = jnp.maximum(jnp.maximum(maxes[0], maxes[1]), maxes[2])
    es = [jnp.exp2(m - mx) for m in maxes]
    inv = 1.0 / (es[0] * dens[0] + es[1] * dens[1] + es[2] * dens[2])
    ws = [e * inv for e in es]
    for h in range(ATTN_HEADS):
        sl = slice(h * ATTN_HEAD_DIM, (h + 1) * ATTN_HEAD_DIM)
        y = None
        for ref, d, pm, w in zip(o_refs, dils, perms, ws):
            if d == 1:
                o = ref[0, :, sl].astype(F32)
            else:
                o = jnp.dot(pm, jnp.concatenate([ref[r, :, sl] for r in range(d)], axis=0),
                            preferred_element_type=F32)
            term = jnp.broadcast_to(w[:, h:h + 1], (tm, ATTN_HEAD_DIM)) * o
            y = term if y is None else y + term
        out_ref[:, sl] = (y * _silu(z_ref[:, sl].astype(F32))).astype(out_ref.dtype)


def _attn_mix(os_, ls_, rest, *, tm):
    b, t, _ = rest.shape
    tm = min(tm, t)
    dils = tuple(o.shape[1] for o in os_)
    grp = lambda d, w: pl.BlockSpec((None, d, tm // d, w), lambda bi, i: (bi, 0, i, 0))
    tok = pl.BlockSpec((None, tm, TOK_WIDTH), lambda bi, i: (bi, i, 0))
    return pl.pallas_call(
        functools.partial(_attn_mix_kernel, dils=dils),
        grid=(b, t // tm),
        in_specs=[grp(d, TOK_WIDTH) for d in dils] + [grp(d, LANES) for d in dils] + [tok],
        out_specs=tok,
        out_shape=jax.ShapeDtypeStruct((b, t, TOK_WIDTH), BF16),
        compiler_params=_cparams("parallel", "parallel"),
        name="attn_mix",
    )(*os_, *ls_, rest)


def _mem_kv(mem, mem_norm_g, w_mem_kv):
    b, n, dm = mem.shape
    (hm,) = _norm(mem.reshape(1, b * n, dm), mem_norm_g)
    kv = _proj(hm, w_mem_kv.astype(BF16), tn=1024, out_dtype=BF16)
    return kv.reshape(b, n, 2 * MEM_WIDTH)


def _finish_layer(x2, gated_tok, gated_mem, w_out, next_g, mode):
    m = x2.shape[0]
    wt = w_out[:TOK_WIDTH].astype(BF16)
    wm = w_out[TOK_WIDTH:].astype(BF16)
    return _out_proj(gated_tok.reshape(m, TOK_WIDTH), gated_mem.reshape(m, MEM_WIDTH), wt, wm, x2,
                     next_g, mode=mode)


def _ssd_layer(x2, h, bt, kv, norm_g, w_in, conv_w, conv_b, dt_bias, a_log, d_skip, ssd_norm_g, w_out,
               next_g=None, mode="plain"):
    b, t = bt
    o_dt = SSD_CONV_DIM
    o_q = o_dt + SSD_HEADS
    o_z = o_q + MEM_WIDTH
    w_main = jnp.concatenate([w_in[:, :o_dt], w_in[:, o_q:]], axis=1).astype(BF16)
    w_dt = jnp.pad(w_in[:, o_dt:o_q], ((0, 0), (0, LANES - SSD_HEADS))).astype(BF16)
    if h is None:
        (h,) = _norm(x2.reshape(b, t, -1), norm_g)
    proj = _proj(h, w_main, tn=2048, out_dtype=BF16).reshape(b, t, -1)
    dt_raw = _proj(h, w_dt, tn=LANES, out_dtype=F32).reshape(b, t, LANES)
    gated_tok = _ssd_mixer(proj, dt_raw, conv_w, conv_b, dt_bias, a_log, d_skip, ssd_norm_g)
    gated_mem = _mem_attn(proj, kv, q_block=SSD_CONV_DIM // MEM_WIDTH,
                          z_block=(SSD_CONV_DIM + MEM_WIDTH + TOK_WIDTH) // MEM_WIDTH, tq=512)
    return _finish_layer(x2, gated_tok, gated_mem, w_out, next_g, mode)


def _attn_layer(x2, bt, kv, norm_g, w_in, w_out, next_g=None, mode="plain"):
    b, t = bt
    o_q = N_DIL * ATTN_GROUP_COLS
    o_z = o_q + MEM_WIDTH
    x3 = x2.reshape(b, t, -1)
    w_rest = jnp.concatenate([w_in[:, o_z:o_z + TOK_WIDTH], w_in[:, o_q:o_z], w_in[:, o_z + TOK_WIDTH:]],
                             axis=1).astype(BF16)
    dils = tuple(d for _, d in DILATED_GROUPS)
    hs = dict(zip(dils, _norm(x3, norm_g, dils)))
    rest = _proj(hs[1], w_rest, tn=2560, out_dtype=BF16).reshape(b, t, -1)
    slopes = jnp.exp2(-ALIBI_MAX_EXP * jnp.arange(1, N_DIL * ATTN_HEADS + 1, dtype=F32) / (N_DIL * ATTN_HEADS))
    slopes = slopes.reshape(N_DIL, ATTN_HEADS)
    q_scale = (ATTN_HEAD_DIM ** -0.5) * LOG2E
    outs, lses = [], []
    col = jnp.arange(w_in.shape[1])
    is_q = (col < o_q) & (col % ATTN_GROUP_COLS < TOK_WIDTH)
    w_all = (w_in * jnp.where(is_q, q_scale, 1.0).astype(F32)).astype(BF16)
    for g, (window, d) in enumerate(DILATED_GROUPS):
        assert window // d == ATTN_BLOCK
        qkv = _proj(hs[d], w_all, tn=2304, out_dtype=BF16, dil=d,
                    col0=g * ATTN_GROUP_COLS, ncols=ATTN_GROUP_COLS)
        o, lse = _dil_attn(qkv, slopes[g] * (float(d) * LOG2E))
        outs.append(o)
        lses.append(lse)
    gated_tok = _attn_mix(outs, lses, rest, tm=256)
    gated_mem = _mem_attn(rest, kv, q_block=TOK_WIDTH // MEM_WIDTH,
                          z_block=(TOK_WIDTH + MEM_WIDTH) // MEM_WIDTH, tq=512)
    return _finish_layer(x2, gated_tok, gated_mem, w_out, next_g, mode)


def kernel(x, mem, mem_norm_g, final_norm_g, norm_g_0, w_in_0, conv_w_0, conv_b_0, dt_bias_0, a_log_0, d_skip_0, ssd_norm_g_0, w_mem_kv_0, w_out_0, norm_g_1, w_in_1, w_mem_kv_1, w_out_1, norm_g_2, w_in_2, conv_w_2, conv_b_2, dt_bias_2, a_log_2, d_skip_2, ssd_norm_g_2, w_mem_kv_2, w_out_2, norm_g_3, w_in_3, w_mem_kv_3, w_out_3):
    b, t, dm = x.shape
    bt = (b, t)
    x2 = x.reshape(b * t, dm)
    (x2,) = _ssd_layer(x2, None, bt, _mem_kv(mem, mem_norm_g, w_mem_kv_0), norm_g_0, w_in_0, conv_w_0, conv_b_0,
                       dt_bias_0, a_log_0, d_skip_0, ssd_norm_g_0, w_out_0)
    x2, h2 = _attn_layer(x2, bt, _mem_kv(mem, mem_norm_g, w_mem_kv_1), norm_g_1, w_in_1, w_out_1,
                         next_g=norm_g_2, mode="norm")
    (x2,) = _ssd_layer(x2, h2.reshape(b, t, dm), bt, _mem_kv(mem, mem_norm_g, w_mem_kv_2), norm_g_2, w_in_2,
                       conv_w_2, conv_b_2, dt_bias_2, a_log_2, d_skip_2, ssd_norm_g_2, w_out_2)
    (y,) = _attn_layer(x2, bt, _mem_kv(mem, mem_norm_g, w_mem_kv_3), norm_g_3, w_in_3, w_out_3,
                       next_g=final_norm_g, mode="final")
    return y.reshape(b, t, dm)
```

```python
import functools

import jax
import jax.numpy as jnp
from jax import lax
from jax.experimental import pallas as pl
from jax.experimental.pallas import tpu as pltpu

F32 = jnp.float32
BF16 = jnp.bfloat16
EPS = 1e-6
NEG = -1e30
LOG2E = 1.4426950408889634

LANES = 128
D_MODEL = 2048
N_MEM = 256
MIX_WIDTH = 2 * D_MODEL
MEM_WIDTH = MIX_WIDTH // 4
TOK_WIDTH = MIX_WIDTH - MEM_WIDTH
MEM_HEADS = 4
MEM_HEAD_DIM = MEM_WIDTH // MEM_HEADS

SSD_HEAD_DIM = 64
SSD_HEADS = TOK_WIDTH // SSD_HEAD_DIM
SSD_GROUPS = 8
SSD_HEADS_PER_GROUP = SSD_HEADS // SSD_GROUPS
SSD_GROUP_COLS = SSD_HEADS_PER_GROUP * SSD_HEAD_DIM
SSD_PAIRS_PER_GROUP = SSD_GROUP_COLS // LANES
SSD_STATE = 128
SSD_CONV = 4
SSD_CHUNK = 128
SSD_BC_COLS = SSD_GROUPS * SSD_STATE
SSD_CONV_DIM = TOK_WIDTH + 2 * SSD_BC_COLS
HALO_ROWS = 8

ATTN_HEAD_DIM = 128
ATTN_HEADS = TOK_WIDTH // ATTN_HEAD_DIM
DILATED_GROUPS = ((128, 1), (512, 4), (2048, 16))
N_DIL = len(DILATED_GROUPS)
ATTN_GROUP_COLS = 3 * TOK_WIDTH
ATTN_BLOCK = 128
ALIBI_MAX_EXP = 8.0

VMEM_LIMIT = 56 * 1024 * 1024


def _cparams(*sem):
    return pltpu.CompilerParams(dimension_semantics=sem, vmem_limit_bytes=VMEM_LIMIT)


def _silu(v):
    hv = 0.5 * v
    return hv + hv * jnp.tanh(hv)


def _split3(v):
    hi = v.astype(BF16)
    r1 = v - hi.astype(F32)
    mid = r1.astype(BF16)
    lo = (r1 - mid.astype(F32)).astype(BF16)
    return hi, mid, lo


ROW_TILE = 1024
PERM_CHUNK = 256
NORM_ROWS = 32


def _norm_kernel(x_ref, g_ref, *out_refs, dils):
    tm = x_ref.shape[0]
    nat_ref = out_refs[0]

    def body(i, carry):
        r = pl.multiple_of(i * NORM_ROWS, NORM_ROWS)
        xf = x_ref[pl.ds(r, NORM_ROWS), :]
        ms = jnp.mean(xf * xf, axis=-1, keepdims=True)
        nat_ref[pl.ds(r, NORM_ROWS), :] = ((xf * lax.rsqrt(ms + EPS)) * g_ref[...]).astype(BF16)
        return carry
    lax.fori_loop(0, tm // NORM_ROWS, body, 0)

    row = lax.broadcasted_iota(jnp.int32, (PERM_CHUNK, PERM_CHUNK), 0)
    col = lax.broadcasted_iota(jnp.int32, (PERM_CHUNK, PERM_CHUNK), 1)
    for d, o_ref in zip(dils[1:], out_refs[1:]):
        sub = tm // d
        csub = PERM_CHUNK // d
        perm = jnp.where((row % csub) * d + row // csub == col, 1.0, 0.0).astype(BF16)
        for c in range(tm // PERM_CHUNK):
            hp = jnp.dot(perm, nat_ref[c * PERM_CHUNK:(c + 1) * PERM_CHUNK, :],
                         preferred_element_type=F32).astype(BF16)
            for r in range(d):
                o_ref[r * sub + c * csub:r * sub + (c + 1) * csub, :] = hp[r * csub:(r + 1) * csub, :]


def _norm(x, g, dils=(1,)):
    b, t, k = x.shape
    tm = min(ROW_TILE, t)
    assert dils[0] == 1 and t % tm == 0 and tm % PERM_CHUNK == 0
    blk = pl.BlockSpec((None, tm, k), lambda bi, i: (bi, i, 0))
    return pl.pallas_call(
        functools.partial(_norm_kernel, dils=dils),
        grid=(b, t // tm),
        in_specs=[blk, pl.BlockSpec((1, k), lambda bi, i: (0, 0))],
        out_specs=[blk] * len(dils),
        out_shape=[jax.ShapeDtypeStruct((b, t, k), BF16)] * len(dils),
        compiler_params=_cparams("parallel", "parallel"),
        name="pre_norm",
    )(x, g.reshape(1, k))


def _proj_kernel(h_ref, w_ref, o_ref):
    res = jnp.dot(h_ref[...], w_ref[...], preferred_element_type=F32).astype(o_ref.dtype)
    o_ref[...] = res.reshape(o_ref.shape)


def _proj(h, w, *, tn, out_dtype, dil=1, col0=0, ncols=None):
    b, t, k = h.shape
    n = w.shape[1] if ncols is None else ncols
    tm = min(ROW_TILE, t)
    tn = min(tn, n)
    assert t % tm == 0 and n % tn == 0 and col0 % tn == 0 and tm % dil == 0
    j0 = col0 // tn
    return pl.pallas_call(
        _proj_kernel,
        grid=(b, t // tm, n // tn),
        in_specs=[pl.BlockSpec((None, tm, k), lambda bi, i, j: (bi, i, 0)),
                  pl.BlockSpec((k, tn), lambda bi, i, j: (0, j0 + j))],
        out_specs=pl.BlockSpec((None, dil, tm // dil, tn), lambda bi, i, j: (bi, 0, i, j)),
        out_shape=jax.ShapeDtypeStruct((b, dil, t // dil, n), out_dtype),
        compiler_params=_cparams("parallel", "parallel", "parallel"),
        name="proj",
    )(h, w)


OUT_ROWS = 512


def _out_proj_kernel(gt_ref, gm_ref, wt_ref, wm_ref, x_ref, *rest, mode):
    o_ref = rest[-1] if mode != "norm" else rest[-2]
    acc = jnp.dot(gt_ref[...], wt_ref[...], preferred_element_type=F32)
    acc = acc + jnp.dot(gm_ref[...], wm_ref[...], preferred_element_type=F32)
    o_ref[...] = x_ref[...] + acc
    if mode == "plain":
        return
    g_ref = rest[0]

    def body(i, carry):
        r = pl.multiple_of(i * NORM_ROWS, NORM_ROWS)
        xf = o_ref[pl.ds(r, NORM_ROWS), :]
        ms = jnp.mean(xf * xf, axis=-1, keepdims=True)
        hn = (xf * lax.rsqrt(ms + EPS)) * g_ref[...]
        if mode == "norm":
            rest[-1][pl.ds(r, NORM_ROWS), :] = hn.astype(BF16)
        else:
            o_ref[pl.ds(r, NORM_ROWS), :] = hn
        return carry
    lax.fori_loop(0, o_ref.shape[0] // NORM_ROWS, body, 0)


def _out_proj(gt, gm, wt, wm, x, g=None, *, mode):
    m, n = x.shape
    tm = min(OUT_ROWS, m)
    assert m % tm == 0 and tm % NORM_ROWS == 0 and (g is None) == (mode == "plain")
    rows = lambda width: pl.BlockSpec((tm, width), lambda i: (i, 0))
    resident = lambda arr: pl.BlockSpec(arr.shape, lambda i: (0, 0), pipeline_mode=pl.Buffered(1))
    in_specs = [rows(gt.shape[1]), rows(gm.shape[1]), resident(wt), resident(wm), rows(n)]
    args = [gt, gm, wt, wm, x]
    if g is not None:
        in_specs.append(pl.BlockSpec((1, n), lambda i: (0, 0)))
        args.append(g.reshape(1, n))
    out_shape = [jax.ShapeDtypeStruct((m, n), F32)]
    if mode == "norm":
        out_shape.append(jax.ShapeDtypeStruct((m, n), BF16))
    return pl.pallas_call(
        functools.partial(_out_proj_kernel, mode=mode),
        grid=(m // tm,),
        in_specs=in_specs,
        out_specs=[rows(n)] * len(out_shape),
        out_shape=out_shape,
        compiler_params=_cparams("parallel"),
        name="out_proj",
    )(*args)


def _mem_attn_kernel(q_ref, z_ref, mk_ref, mv_ref, o_ref):
    scale = MEM_HEAD_DIM ** -0.5
    for h in range(MEM_HEADS):
        sl = slice(h * MEM_HEAD_DIM, (h + 1) * MEM_HEAD_DIM)
        s = lax.dot_general(q_ref[:, sl], mk_ref[:, sl], (((1,), (1,)), ((), ())),
                            preferred_element_type=F32) * scale
        m = jnp.max(s, axis=-1, keepdims=True)
        p = jnp.exp(s - m)
        den = jnp.sum(p, axis=-1, keepdims=True)
        o = jnp.dot(p.astype(BF16), mv_ref[:, sl], preferred_element_type=F32) / den
        z = z_ref[:, sl].astype(F32)
        o_ref[:, sl] = (o * _silu(z)).astype(o_ref.dtype)


def _mem_attn(proj, kv, *, q_block, z_block, tq):
    b, t, _ = proj.shape
    tq = min(tq, t)
    return pl.pallas_call(
        _mem_attn_kernel,
        grid=(b, t // tq),
        in_specs=[pl.BlockSpec((None, tq, MEM_WIDTH), lambda bi, i: (bi, i, q_block)),
                  pl.BlockSpec((None, tq, MEM_WIDTH), lambda bi, i: (bi, i, z_block)),
                  pl.BlockSpec((None, N_MEM, MEM_WIDTH), lambda bi, i: (bi, 0, 0)),
                  pl.BlockSpec((None, N_MEM, MEM_WIDTH), lambda bi, i: (bi, 0, 1))],
        out_specs=pl.BlockSpec((None, tq, MEM_WIDTH), lambda bi, i: (bi, i, 0)),
        out_shape=jax.ShapeDtypeStruct((b, t, MEM_WIDTH), BF16),
        compiler_params=_cparams("parallel", "parallel"),
        name="mem_attn",
    )(proj, proj, kv, kv)


def _ssd_kernel(xbc_ref, z_ref, dt_ref, cw_ref, cb_ref, dtb_ref, alog_ref, dskip_ref, ng_ref,
                sel_head_ref, sel_pair_ref, o_ref, halo_ref, xc_ref, state_ref):
    q = SSD_CHUNK
    c = pl.program_id(1)

    @pl.when(c == 0)
    def _():
        halo_ref[...] = jnp.zeros_like(halo_ref)
        state_ref[...] = jnp.zeros_like(state_ref)

    for t in range(SSD_CONV_DIM // LANES):
        cs = slice(t * LANES, (t + 1) * LANES)
        u = xbc_ref[:, cs].astype(F32)
        ext = jnp.concatenate([halo_ref[:, cs], u], axis=0)
        acc = cb_ref[:, cs] + cw_ref[SSD_CONV - 1:SSD_CONV, cs] * u
        for k in range(SSD_CONV - 1):
            shifted = pltpu.roll(ext, SSD_CONV - 1 - k, axis=0)[HALO_ROWS:, :]
            acc = acc + cw_ref[k:k + 1, cs] * shifted
        xc_ref[:, cs] = _silu(acc)
        halo_ref[:, cs] = u[q - HALO_ROWS:, :]

    dtr = dt_ref[...] + dtb_ref[...]
    dt = jnp.maximum(dtr, 0.0) + jnp.log1p(jnp.exp(-jnp.abs(dtr)))
    a = -jnp.exp(alog_ref[...])
    dta = dt * a
    row = lax.broadcasted_iota(jnp.int32, (q, q), 0)
    col = lax.broadcasted_iota(jnp.int32, (q, q), 1)
    causal = row >= col
    tri = jnp.where(causal, 1.0, 0.0).astype(BF16)
    hi, mid, lo = _split3(dta)
    a_cs = (jnp.dot(tri, hi, preferred_element_type=F32)
            + jnp.dot(tri, mid, preferred_element_type=F32)
            + jnp.dot(tri, lo, preferred_element_type=F32))
    a_cs_t = a_cs.T
    dt_t = dt.T
    a_last = a_cs[q - 1:q, :]
    lane = lax.broadcasted_iota(jnp.int32, (q, LANES), 1)
    first_head = lane < SSD_HEAD_DIM
    acs_parts = _split3(a_cs)
    eacs_parts = _split3(jnp.exp(a_cs))
    wst_parts = _split3(jnp.exp(a_last - a_cs) * dt)

    def spread(parts, sel_ref, cols):
        return sum(jnp.dot(part, sel_ref[:, cols], preferred_element_type=F32) for part in parts)

    for g in range(SSD_GROUPS):
        head_cols = slice(g * SSD_HEADS_PER_GROUP * LANES, (g + 1) * SSD_HEADS_PER_GROUP * LANES)
        pair_cols = slice(g * SSD_GROUP_COLS, (g + 1) * SSD_GROUP_COLS)
        acs_heads = spread(acs_parts, sel_head_ref, head_cols)
        eacs_pairs = spread(eacs_parts, sel_pair_ref, pair_cols)
        wst_pairs = spread(wst_parts, sel_pair_ref, pair_cols)
        b_off = TOK_WIDTH + g * SSD_STATE
        c_off = TOK_WIDTH + SSD_BC_COLS + g * SSD_STATE
        bm = xc_ref[:, b_off:b_off + SSD_STATE]
        cm = xc_ref[:, c_off:c_off + SSD_STATE].astype(BF16)
        bm_t = bm.T.astype(BF16)
        cb = lax.dot_general(cm, bm.astype(BF16), (((1,), (1,)), ((), ())),
                             preferred_element_type=F32)
        sumsq = jnp.zeros((q, 1), F32)
        gated = []
        for pr in range(SSD_PAIRS_PER_GROUP):
            h0 = g * SSD_HEADS_PER_GROUP + 2 * pr
            xs = slice(g * SSD_GROUP_COLS + pr * LANES, g * SSD_GROUP_COLS + (pr + 1) * LANES)
            ss = slice(pr * LANES, (pr + 1) * LANES)
            xp = xc_ref[:, xs]
            xpb = xp.astype(BF16)
            res = []
            for h in (h0, h0 + 1):
                hh = h - g * SSD_HEADS_PER_GROUP
                seg = acs_heads[:, hh * LANES:(hh + 1) * LANES] - a_cs_t[h:h + 1, :]
                decay = jnp.exp(jnp.where(causal, seg, NEG))
                mat = (cb * decay * dt_t[h:h + 1, :]).astype(BF16)
                res.append(jnp.dot(mat, xpb, preferred_element_type=F32))
            y = jnp.where(first_head, res[0], res[1])
            s_prev = state_ref[g, :, ss]
            e_pair = eacs_pairs[:, ss]
            y = y + jnp.dot(cm, s_prev.astype(BF16), preferred_element_type=F32) * e_pair
            y = y + dskip_ref[:, xs] * xp
            xw = (xp * wst_pairs[:, ss]).astype(BF16)
            chunk_decay = e_pair[q - 1:q, :]
            state_ref[g, :, ss] = chunk_decay * s_prev + jnp.dot(bm_t, xw, preferred_element_type=F32)
            z = z_ref[:, xs].astype(F32)
            gz = y * _silu(z)
            sumsq = sumsq + jnp.sum(gz * gz, axis=-1, keepdims=True)
            gated.append(gz)
        rinv = lax.rsqrt(sumsq * (1.0 / SSD_GROUP_COLS) + EPS)
        for pr in range(SSD_PAIRS_PER_GROUP):
            xs = slice(g * SSD_GROUP_COLS + pr * LANES, g * SSD_GROUP_COLS + (pr + 1) * LANES)
            o_ref[:, xs] = ((gated[pr] * rinv) * ng_ref[:, xs]).astype(o_ref.dtype)


def _ssd_mixer(proj, dt_raw, conv_w, conv_b, dt_bias, a_log, d_skip, ssd_norm_g):
    b, t, _ = proj.shape
    q = SSD_CHUNK
    pad = LANES - SSD_HEADS
    full = lambda bi, ci: (0, 0)
    head_of_row = jnp.arange(LANES)[:, None]
    return pl.pallas_call(
        _ssd_kernel,
        grid=(b, t // q),
        in_specs=[pl.BlockSpec((None, q, SSD_CONV_DIM), lambda bi, ci: (bi, ci, 0)),
                  pl.BlockSpec((None, q, TOK_WIDTH), lambda bi, ci: (bi, ci, 2)),
                  pl.BlockSpec((None, q, LANES), lambda bi, ci: (bi, ci, 0)),
                  pl.BlockSpec((SSD_CONV, SSD_CONV_DIM), full),
                  pl.BlockSpec((1, SSD_CONV_DIM), full),
                  pl.BlockSpec((1, LANES), full),
                  pl.BlockSpec((1, LANES), full),
                  pl.BlockSpec((1, TOK_WIDTH), full),
                  pl.BlockSpec((1, TOK_WIDTH), full),
                  pl.BlockSpec((LANES, SSD_HEADS * LANES), full),
                  pl.BlockSpec((LANES, TOK_WIDTH), full)],
        out_specs=pl.BlockSpec((None, q, TOK_WIDTH), lambda bi, ci: (bi, ci, 0)),
        out_shape=jax.ShapeDtypeStruct((b, t, TOK_WIDTH), BF16),
        scratch_shapes=[pltpu.VMEM((HALO_ROWS, SSD_CONV_DIM), F32),
                        pltpu.VMEM((q, SSD_CONV_DIM), F32),
                        pltpu.VMEM((SSD_GROUPS, SSD_STATE, SSD_GROUP_COLS), F32)],
        compiler_params=_cparams("parallel", "arbitrary"),
        name="ssd_mixer",
    )(proj, proj, dt_raw, conv_w, conv_b.reshape(1, -1),
      jnp.pad(dt_bias, (0, pad)).reshape(1, LANES), jnp.pad(a_log, (0, pad)).reshape(1, LANES),
      jnp.repeat(d_skip, SSD_HEAD_DIM).reshape(1, TOK_WIDTH), ssd_norm_g.reshape(1, TOK_WIDTH),
      (head_of_row == jnp.arange(SSD_HEADS * LANES)[None, :] // LANES).astype(BF16),
      (head_of_row == jnp.arange(TOK_WIDTH)[None, :] // SSD_HEAD_DIM).astype(BF16))


ATTN_SLOTS = 4


def _dil_attn_kernel(slope_ref, q_ref, k_ref, v_ref, o_ref, lse_ref,
                     bias_ref, kt_ref, ktp_ref, vp_ref, s_ref, p_ref):
    c = ATTN_BLOCK
    n = pl.program_id(2)

    @pl.when((pl.program_id(0) == 0) & (pl.program_id(1) == 0) & (n == 0))
    def _():
        row = lax.broadcasted_iota(jnp.int32, (c, 2 * c), 0)
        col = lax.broadcasted_iota(jnp.int32, (c, 2 * c), 1)
        rel_pc = row + c - col
        rel_cp = jnp.where(col < c, row - col, row + 2 * c - col)
        band = lambda rel: (rel >= 0) & (rel <= c)
        for h in range(ATTN_HEADS):
            slope = slope_ref[h]
            bias_ref[0, h] = jnp.where(band(rel_pc), -slope * rel_pc.astype(F32), NEG)
            bias_ref[1, h] = jnp.where(band(rel_cp), -slope * rel_cp.astype(F32), NEG)
            bias_ref[2, h] = jnp.where(band(rel_cp) & (col < c), -slope * rel_cp.astype(F32), NEG)
        ktp_ref[...] = jnp.zeros_like(ktp_ref)
        vp_ref[...] = jnp.zeros_like(vp_ref)

    table_a = jnp.where(n == 0, 2, 1)
    lane = lax.broadcasted_iota(jnp.int32, (c, LANES), 1)
    lse_ref[...] = jnp.zeros_like(lse_ref)

    def head_cols(h):
        return slice(h * ATTN_HEAD_DIM, (h + 1) * ATTN_HEAD_DIM)

    def transpose_keys(blk, h):
        kt_ref[blk, head_cols(h), :] = k_ref[blk * c:(blk + 1) * c, head_cols(h)].T

    def scores(blk, h, slot):
        kt_a = kt_ref[0, head_cols(h), :]
        if blk == 0:
            kt_w = jnp.concatenate([kt_a, ktp_ref[head_cols(h), :]], axis=1)
            bias = bias_ref[table_a, h]
        else:
            kt_w = jnp.concatenate([kt_a, kt_ref[1, head_cols(h), :]], axis=1)
            bias = bias_ref[0, h]
        s = jnp.dot(q_ref[blk * c:(blk + 1) * c, head_cols(h)], kt_w, preferred_element_type=F32)
        s_ref[slot] = s + bias

    def softmax(blk, h, slot):
        s = s_ref[slot]
        m = jnp.max(s, axis=-1, keepdims=True)
        p = jnp.exp2(s - m)
        den = jnp.sum(p, axis=-1, keepdims=True)
        p_ref[slot] = p.astype(BF16)
        rows = slice(blk * c, (blk + 1) * c)
        lse_ref[rows, :] = jnp.where(lane == h, m, jnp.where(lane == ATTN_HEADS + h, den, lse_ref[rows, :]))

    def values(blk, h, slot):
        if blk == 0:
            vw = jnp.concatenate([v_ref[0:c, head_cols(h)], vp_ref[:, head_cols(h)]], axis=0)
        else:
            vw = v_ref[:, head_cols(h)]
        o_ref[blk * c:(blk + 1) * c, head_cols(h)] = jnp.dot(
            p_ref[slot], vw, preferred_element_type=F32).astype(o_ref.dtype)

    for blk in range(2):
        for h in range(ATTN_HEADS):
            transpose_keys(blk, h)

    items = [(blk, h) for blk in range(2) for h in range(ATTN_HEADS)]
    for i in range(len(items) + 2):
        if i < len(items):
            scores(*items[i], i % ATTN_SLOTS)
        if 1 <= i <= len(items):
            softmax(*items[i - 1], (i - 1) % ATTN_SLOTS)
        if i >= 2:
            values(*items[i - 2], (i - 2) % ATTN_SLOTS)

    ktp_ref[...] = kt_ref[1]
    vp_ref[...] = v_ref[c:2 * c, :]


def _dil_attn(qkv, slopes_d):
    b, d, n_sub, _ = qkv.shape
    c = ATTN_BLOCK
    assert n_sub % (2 * c) == 0
    blk = (None, None, 2 * c, TOK_WIDTH)
    col_block = lambda off: (lambda bi, r, n, *_: (bi, r, n, off))
    return pl.pallas_call(
        _dil_attn_kernel,
        grid_spec=pltpu.PrefetchScalarGridSpec(
            num_scalar_prefetch=1,
            grid=(b, d, n_sub // (2 * c)),
            in_specs=[pl.BlockSpec(blk, col_block(0)), pl.BlockSpec(blk, col_block(1)),
                      pl.BlockSpec(blk, col_block(2))],
            out_specs=[pl.BlockSpec(blk, col_block(0)),
                       pl.BlockSpec((None, None, 2 * c, LANES), col_block(0))],
            scratch_shapes=[pltpu.VMEM((3, ATTN_HEADS, c, 2 * c), F32),
                            pltpu.VMEM((2, TOK_WIDTH, c), BF16),
                            pltpu.VMEM((TOK_WIDTH, c), BF16),
                            pltpu.VMEM((c, TOK_WIDTH), BF16),
                            pltpu.VMEM((ATTN_SLOTS, c, 2 * c), F32),
                            pltpu.VMEM((ATTN_SLOTS, c, 2 * c), BF16)]),
        out_shape=[jax.ShapeDtypeStruct((b, d, n_sub, TOK_WIDTH), BF16),
                   jax.ShapeDtypeStruct((b, d, n_sub, LANES), F32)],
        compiler_params=_cparams("arbitrary", "arbitrary", "arbitrary"),
        name="dil_attn",
    )(slopes_d, qkv, qkv, qkv)


def _attn_mix_kernel(o0_ref, o1_ref, o2_ref, l0_ref, l1_ref, l2_ref, z_ref, out_ref, *, dils):
    tm = out_ref.shape[0]
    row = lax.broadcasted_iota(jnp.int32, (tm, tm), 0)
    col = lax.broadcasted_iota(jnp.int32, (tm, tm), 1)

    def perm(d):
        sub = tm // d
        return jnp.where((row % d) * sub + row // d == col, 1.0, 0.0).astype(BF16)

    def stacked(ref, d):
        return ref[0] if d == 1 else jnp.concatenate([ref[r] for r in range(d)], axis=0)

    perms = [None if d == 1 else perm(d) for d in dils]
    o_refs = (o0_ref, o1_ref, o2_ref)
    maxes, dens = [], []
    for ref, d, pm in zip((l0_ref, l1_ref, l2_ref), dils, perms):
        st = stacked(ref, d)
        if pm is not None:
            st = sum(jnp.dot(pm, part, preferred_element_type=F32) for part in _split3(st))
        maxes.append(st)
        dens.append(pltpu.roll(st, LANES - ATTN_HEADS, axis=1))
    mx = jnp.maximum(jnp.maximum(maxes[0], maxes[1]), maxes[2])
    es = [jnp.exp2(m - mx) for m in maxes]
    inv = 1.0 / (es[0] * dens[0] + es[1] * dens[1] + es[2] * dens[2])
    ws = [e * inv for e in es]
    for h in range(ATTN_HEADS):
        sl = slice(h * ATTN_HEAD_DIM, (h + 1) * ATTN_HEAD_DIM)
        y = None
        for ref, d, pm, w in zip(o_refs, dils, perms, ws):
            if d == 1:
                o = ref[0, :, sl].astype(F32)
            else:
                o = jnp.dot(pm, jnp.concatenate([ref[r, :, sl] for r in range(d)], axis=0),
                            preferred_element_type=F32)
            term = jnp.broadcast_to(w[:, h:h + 1], (tm, ATTN_HEAD_DIM)) * o
            y = term if y is None else y + term
        out_ref[:, sl] = (y * _silu(z_ref[:, sl].astype(F32))).astype(out_ref.dtype)


def _attn_mix(os_, ls_, rest, *, tm):
    b, t, _ = rest.shape
    tm = min(tm, t)
    dils = tuple(o.shape[1] for o in os_)
    grp = lambda d, w: pl.BlockSpec((None, d, tm // d, w), lambda bi, i: (bi, 0, i, 0))
    tok = pl.BlockSpec((None, tm, TOK_WIDTH), lambda bi, i: (bi, i, 0))
    return pl.pallas_call(
        functools.partial(_attn_mix_kernel, dils=dils),
        grid=(b, t // tm),
        in_specs=[grp(d, TOK_WIDTH) for d in dils] + [grp(d, LANES) for d in dils] + [tok],
        out_specs=tok,
        out_shape=jax.ShapeDtypeStruct((b, t, TOK_WIDTH), BF16),
        compiler_params=_cparams("parallel", "parallel"),
        name="attn_mix",
    )(*os_, *ls_, rest)


def _mem_kv(mem, mem_norm_g, w_mem_kv):
    b, n, dm = mem.shape
    (hm,) = _norm(mem.reshape(1, b * n, dm), mem_norm_g)
    kv = _proj(hm, w_mem_kv.astype(BF16), tn=1024, out_dtype=BF16)
    return kv.reshape(b, n, 2 * MEM_WIDTH)


def _finish_layer(x2, gated_tok, gated_mem, w_out, next_g, mode):
    m = x2.shape[0]
    wt = w_out[:TOK_WIDTH].astype(BF16)
    wm = w_out[TOK_WIDTH:].astype(BF16)
    return _out_proj(gated_tok.reshape(m, TOK_WIDTH), gated_mem.reshape(m, MEM_WIDTH), wt, wm, x2,
                     next_g, mode=mode)


def _ssd_layer(x2, h, bt, kv, norm_g, w_in, conv_w, conv_b, dt_bias, a_log, d_skip, ssd_norm_g, w_out,
               next_g=None, mode="plain"):
    b, t = bt
    o_dt = SSD_CONV_DIM
    o_q = o_dt + SSD_HEADS
    o_z = o_q + MEM_WIDTH
    w_main = jnp.concatenate([w_in[:, :o_dt], w_in[:, o_q:]], axis=1).astype(BF16)
    w_dt = jnp.pad(w_in[:, o_dt:o_q], ((0, 0), (0, LANES - SSD_HEADS))).astype(BF16)
    if h is None:
        (h,) = _norm(x2.reshape(b, t, -1), norm_g)
    proj = _proj(h, w_main, tn=2048, out_dtype=BF16).reshape(b, t, -1)
    dt_raw = _proj(h, w_dt, tn=LANES, out_dtype=F32).reshape(b, t, LANES)
    gated_tok = _ssd_mixer(proj, dt_raw, conv_w, conv_b, dt_bias, a_log, d_skip, ssd_norm_g)
    gated_mem = _mem_attn(proj, kv, q_block=SSD_CONV_DIM // MEM_WIDTH,
                          z_block=(SSD_CONV_DIM + MEM_WIDTH + TOK_WIDTH) // MEM_WIDTH, tq=512)
    return _finish_layer(x2, gated_tok, gated_mem, w_out, next_g, mode)


def _attn_layer(x2, bt, kv, norm_g, w_in, w_out, next_g=None, mode="plain"):
    b, t = bt
    o_q = N_DIL * ATTN_GROUP_COLS
    o_z = o_q + MEM_WIDTH
    x3 = x2.reshape(b, t, -1)
    w_rest = jnp.concatenate([w_in[:, o_z:o_z + TOK_WIDTH], w_in[:, o_q:o_z], w_in[:, o_z + TOK_WIDTH:]],
                             axis=1).astype(BF16)
    dils = tuple(d for _, d in DILATED_GROUPS)
    hs = dict(zip(dils, _norm(x3, norm_g, dils)))
    rest = _proj(hs[1], w_rest, tn=2560, out_dtype=BF16).reshape(b, t, -1)
    slopes = jnp.exp2(-ALIBI_MAX_EXP * jnp.arange(1, N_DIL * ATTN_HEADS + 1, dtype=F32) / (N_DIL * ATTN_HEADS))
    slopes = slopes.reshape(N_DIL, ATTN_HEADS)
    q_scale = (ATTN_HEAD_DIM ** -0.5) * LOG2E
    outs, lses = [], []
    col = jnp.arange(w_in.shape[1])
    is_q = (col < o_q) & (col % ATTN_GROUP_COLS < TOK_WIDTH)
    w_all = (w_in * jnp.where(is_q, q_scale, 1.0).astype(F32)).astype(BF16)
    for g, (window, d) in enumerate(DILATED_GROUPS):
        assert window // d == ATTN_BLOCK
        qkv = _proj(hs[d], w_all, tn=2304, out_dtype=BF16, dil=d,
                    col0=g * ATTN_GROUP_COLS, ncols=ATTN_GROUP_COLS)
        o, lse = _dil_attn(qkv, slopes[g] * (float(d) * LOG2E))
        outs.append(o)
        lses.append(lse)
    gated_tok = _attn_mix(outs, lses, rest, tm=256)
    gated_mem = _mem_attn(rest, kv, q_block=TOK_WIDTH // MEM_WIDTH,
                          z_block=(TOK_WIDTH + MEM_WIDTH) // MEM_WIDTH, tq=512)
    return _finish_layer(x2, gated_tok, gated_mem, w_out, next_g, mode)


def kernel(x, mem, mem_norm_g, final_norm_g, norm_g_0, w_in_0, conv_w_0, conv_b_0, dt_bias_0, a_log_0, d_skip_0, ssd_norm_g_0, w_mem_kv_0, w_out_0, norm_g_1, w_in_1, w_mem_kv_1, w_out_1, norm_g_2, w_in_2, conv_w_2, conv_b_2, dt_bias_2, a_log_2, d_skip_2, ssd_norm_g_2, w_mem_kv_2, w_out_2, norm_g_3, w_in_3, w_mem_kv_3, w_out_3):
    b, t, dm = x.shape
    bt = (b, t)
    x2 = x.reshape(b * t, dm)
    (x2,) = _ssd_layer(x2, None, bt, _mem_kv(mem, mem_norm_g, w_mem_kv_0), norm_g_0, w_in_0, conv_w_0, conv_b_0,
                       dt_bias_0, a_log_0, d_skip_0, ssd_norm_g_0, w_out_0)
    x2, h2 = _attn_layer(x2, bt, _mem_kv(mem, mem_norm_g, w_mem_kv_1), norm_g_1, w_in_1, w_out_1,
                         next_g=norm_g_2, mode="norm")
    (x2,) = _ssd_layer(x2, h2.reshape(b, t, dm), bt, _mem_kv(mem, mem_norm_g, w_mem_kv_2), norm_g_2, w_in_2,
                       conv_w_2, conv_b_2, dt_bias_2, a_log_2, d_skip_2, ssd_norm_g_2, w_out_2)
    (y,) = _attn_layer(x2, bt, _mem_kv(mem, mem_norm_g, w_mem_kv_3), norm_g_3, w_in_3, w_out_3,
                       next_g=final_norm_g, mode="final")
    return y.reshape(b, t, dm)
```

```python
import functools

import jax
import jax.numpy as jnp
from jax import lax
from jax.experimental import pallas as pl
from jax.experimental.pallas import tpu as pltpu

F32 = jnp.float32
BF16 = jnp.bfloat16
EPS = 1e-6
NEG = -1e30
LOG2E = 1.4426950408889634

LANES = 128
D_MODEL = 2048
N_MEM = 256
MIX_WIDTH = 2 * D_MODEL
MEM_WIDTH = MIX_WIDTH // 4
TOK_WIDTH = MIX_WIDTH - MEM_WIDTH
MEM_HEADS = 4
MEM_HEAD_DIM = MEM_WIDTH // MEM_HEADS

SSD_HEAD_DIM = 64
SSD_HEADS = TOK_WIDTH // SSD_HEAD_DIM
SSD_GROUPS = 8
SSD_HEADS_PER_GROUP = SSD_HEADS // SSD_GROUPS
SSD_GROUP_COLS = SSD_HEADS_PER_GROUP * SSD_HEAD_DIM
SSD_PAIRS_PER_GROUP = SSD_GROUP_COLS // LANES
SSD_STATE = 128
SSD_CONV = 4
SSD_CHUNK = 128
SSD_BC_COLS = SSD_GROUPS * SSD_STATE
SSD_CONV_DIM = TOK_WIDTH + 2 * SSD_BC_COLS
HALO_ROWS = 8

ATTN_HEAD_DIM = 128
ATTN_HEADS = TOK_WIDTH // ATTN_HEAD_DIM
DILATED_GROUPS = ((128, 1), (512, 4), (2048, 16))
N_DIL = len(DILATED_GROUPS)
ATTN_GROUP_COLS = 3 * TOK_WIDTH
ATTN_BLOCK = 128
ALIBI_MAX_EXP = 8.0

VMEM_LIMIT = 56 * 1024 * 1024


def _cparams(*sem):
    return pltpu.CompilerParams(dimension_semantics=sem, vmem_limit_bytes=VMEM_LIMIT)


def _silu(v):
    hv = 0.5 * v
    return hv + hv * jnp.tanh(hv)


def _split3(v):
    hi = v.astype(BF16)
    r1 = v - hi.astype(F32)
    mid = r1.astype(BF16)
    lo = (r1 - mid.astype(F32)).astype(BF16)
    return hi, mid, lo


ROW_TILE = 1024
PERM_CHUNK = 256
NORM_ROWS = 32


def _norm_kernel(x_ref, g_ref, *out_refs, dils):
    tm = x_ref.shape[0]
    nat_ref = out_refs[0]

    def body(i, carry):
        r = pl.multiple_of(i * NORM_ROWS, NORM_ROWS)
        xf = x_ref[pl.ds(r, NORM_ROWS), :]
        ms = jnp.mean(xf * xf, axis=-1, keepdims=True)
        nat_ref[pl.ds(r, NORM_ROWS), :] = ((xf * lax.rsqrt(ms + EPS)) * g_ref[...]).astype(BF16)
        return carry
    lax.fori_loop(0, tm // NORM_ROWS, body, 0)

    row = lax.broadcasted_iota(jnp.int32, (PERM_CHUNK, PERM_CHUNK), 0)
    col = lax.broadcasted_iota(jnp.int32, (PERM_CHUNK, PERM_CHUNK), 1)
    for d, o_ref in zip(dils[1:], out_refs[1:]):
        sub = tm // d
        csub = PERM_CHUNK // d
        perm = jnp.where((row % csub) * d + row // csub == col, 1.0, 0.0).astype(BF16)
        for c in range(tm // PERM_CHUNK):
            hp = jnp.dot(perm, nat_ref[c * PERM_CHUNK:(c + 1) * PERM_CHUNK, :],
                         preferred_element_type=F32).astype(BF16)
            for r in range(d):
                o_ref[r * sub + c * csub:r * sub + (c + 1) * csub, :] = hp[r * csub:(r + 1) * csub, :]


def _norm(x, g, dils=(1,)):
    b, t, k = x.shape
    tm = min(ROW_TILE, t)
    assert dils[0] == 1 and t % tm == 0 and tm % PERM_CHUNK == 0
    blk = pl.BlockSpec((None, tm, k), lambda bi, i: (bi, i, 0))
    return pl.pallas_call(
        functools.partial(_norm_kernel, dils=dils),
        grid=(b, t // tm),
        in_specs=[blk, pl.BlockSpec((1, k), lambda bi, i: (0, 0))],
        out_specs=[blk] * len(dils),
        out_shape=[jax.ShapeDtypeStruct((b, t, k), BF16)] * len(dils),
        compiler_params=_cparams("parallel", "parallel"),
        name="pre_norm",
    )(x, g.reshape(1, k))


def _proj_kernel(h_ref, w_ref, o_ref):
    res = jnp.dot(h_ref[...], w_ref[...], preferred_element_type=F32).astype(o_ref.dtype)
    o_ref[...] = res.reshape(o_ref.shape)


def _proj(h, w, *, tn, out_dtype, dil=1, col0=0, ncols=None):
    b, t, k = h.shape
    n = w.shape[1] if ncols is None else ncols
    tm = min(ROW_TILE, t)
    tn = min(tn, n)
    assert t % tm == 0 and n % tn == 0 and col0 % tn == 0 and tm % dil == 0
    j0 = col0 // tn
    return pl.pallas_call(
        _proj_kernel,
        grid=(b, t // tm, n // tn),
        in_specs=[pl.BlockSpec((None, tm, k), lambda bi, i, j: (bi, i, 0)),
                  pl.BlockSpec((k, tn), lambda bi, i, j: (0, j0 + j))],
        out_specs=pl.BlockSpec((None, dil, tm // dil, tn), lambda bi, i, j: (bi, 0, i, j)),
        out_shape=jax.ShapeDtypeStruct((b, dil, t // dil, n), out_dtype),
        compiler_params=_cparams("parallel", "parallel", "parallel"),
        name="proj",
    )(h, w)


OUT_ROWS = 512


def _out_proj_kernel(gt_ref, gm_ref, wt_ref, wm_ref, x_ref, *rest, mode):
    o_ref = rest[-1] if mode != "norm" else rest[-2]
    acc = jnp.dot(gt_ref[...], wt_ref[...], preferred_element_type=F32)
    acc = acc + jnp.dot(gm_ref[...], wm_ref[...], preferred_element_type=F32)
    o_ref[...] = x_ref[...] + acc
    if mode == "plain":
        return
    g_ref = rest[0]

    def body(i, carry):
        r = pl.multiple_of(i * NORM_ROWS, NORM_ROWS)
        xf = o_ref[pl.ds(r, NORM_ROWS), :]
        ms = jnp.mean(xf * xf, axis=-1, keepdims=True)
        hn = (xf * lax.rsqrt(ms + EPS)) * g_ref[...]
        if mode == "norm":
            rest[-1][pl.ds(r, NORM_ROWS), :] = hn.astype(BF16)
        else:
            o_ref[pl.ds(r, NORM_ROWS), :] = hn
        return carry
    lax.fori_loop(0, o_ref.shape[0] // NORM_ROWS, body, 0, unroll=2)


def _out_proj(gt, gm, wt, wm, x, g=None, *, mode):
    m, n = x.shape
    tm = min(OUT_ROWS, m)
    assert m % tm == 0 and tm % NORM_ROWS == 0 and (g is None) == (mode == "plain")
    rows = lambda width: pl.BlockSpec((tm, width), lambda i: (i, 0))
    resident = lambda arr: pl.BlockSpec(arr.shape, lambda i: (0, 0), pipeline_mode=pl.Buffered(1))
    in_specs = [rows(gt.shape[1]), rows(gm.shape[1]), resident(wt), resident(wm), rows(n)]
    args = [gt, gm, wt, wm, x]
    if g is not None:
        in_specs.append(pl.BlockSpec((1, n), lambda i: (0, 0)))
        args.append(g.reshape(1, n))
    out_shape = [jax.ShapeDtypeStruct((m, n), F32)]
    if mode == "norm":
        out_shape.append(jax.ShapeDtypeStruct((m, n), BF16))
    return pl.pallas_call(
        functools.partial(_out_proj_kernel, mode=mode),
        grid=(m // tm,),
        in_specs=in_specs,
        out_specs=[rows(n)] * len(out_shape),
        out_shape=out_shape,
        compiler_params=_cparams("parallel"),
        name="out_proj",
    )(*args)


MEM_Q_SCALE = (MEM_HEAD_DIM ** -0.5) * LOG2E


def _mem_attn_kernel(q_ref, z_ref, mk_ref, mv_ref, o_ref):
    for h in range(MEM_HEADS):
        sl = slice(h * MEM_HEAD_DIM, (h + 1) * MEM_HEAD_DIM)
        s = lax.dot_general(q_ref[:, sl], mk_ref[:, sl], (((1,), (1,)), ((), ())),
                            preferred_element_type=F32)
        m = jnp.max(s, axis=-1, keepdims=True)
        p = jnp.exp2(s - m)
        den = jnp.sum(p, axis=-1, keepdims=True)
        o = jnp.dot(p.astype(BF16), mv_ref[:, sl], preferred_element_type=F32) / den
        z = z_ref[:, sl].astype(F32)
        o_ref[:, sl] = (o * _silu(z)).astype(o_ref.dtype)


def _mem_attn(proj, kv, *, q_block, z_block, tq):
    b, t, _ = proj.shape
    tq = min(tq, t)
    return pl.pallas_call(
        _mem_attn_kernel,
        grid=(b, t // tq),
        in_specs=[pl.BlockSpec((None, tq, MEM_WIDTH), lambda bi, i: (bi, i, q_block)),
                  pl.BlockSpec((None, tq, MEM_WIDTH), lambda bi, i: (bi, i, z_block)),
                  pl.BlockSpec((None, N_MEM, MEM_WIDTH), lambda bi, i: (bi, 0, 0)),
                  pl.BlockSpec((None, N_MEM, MEM_WIDTH), lambda bi, i: (bi, 0, 1))],
        out_specs=pl.BlockSpec((None, tq, MEM_WIDTH), lambda bi, i: (bi, i, 0)),
        out_shape=jax.ShapeDtypeStruct((b, t, MEM_WIDTH), BF16),
        compiler_params=_cparams("parallel", "parallel"),
        name="mem_attn",
    )(proj, proj, kv, kv)


def _ssd_kernel(xbc_ref, z_ref, dt_ref, cw_ref, cb_ref, dtb_ref, alog_ref, dskip_ref, ng_ref,
                sel_pair_ref, o_ref, halo_ref, xc_ref, state_ref):
    q = SSD_CHUNK
    c = pl.program_id(1)

    @pl.when(c == 0)
    def _():
        halo_ref[...] = jnp.zeros_like(halo_ref)
        state_ref[...] = jnp.zeros_like(state_ref)

    for t in range(SSD_CONV_DIM // LANES):
        cs = slice(t * LANES, (t + 1) * LANES)
        u = xbc_ref[:, cs].astype(F32)
        ext = jnp.concatenate([halo_ref[:, cs], u], axis=0)
        acc = cb_ref[:, cs] + cw_ref[SSD_CONV - 1:SSD_CONV, cs] * u
        for k in range(SSD_CONV - 1):
            shifted = pltpu.roll(ext, SSD_CONV - 1 - k, axis=0)[HALO_ROWS:, :]
            acc = acc + cw_ref[k:k + 1, cs] * shifted
        xc_ref[:, cs] = _silu(acc)
        halo_ref[:, cs] = u[q - HALO_ROWS:, :]

    dtr = dt_ref[...] + dtb_ref[...]
    dt = jnp.maximum(dtr, 0.0) + jnp.log1p(jnp.exp(-jnp.abs(dtr)))
    a = -jnp.exp(alog_ref[...])
    dta = dt * a
    row = lax.broadcasted_iota(jnp.int32, (q, q), 0)
    col = lax.broadcasted_iota(jnp.int32, (q, q), 1)
    causal = row >= col
    tri = jnp.where(causal, 1.0, 0.0).astype(BF16)
    hi, mid, lo = _split3(dta)
    a_cs = (jnp.dot(tri, hi, preferred_element_type=F32)
            + jnp.dot(tri, mid, preferred_element_type=F32)
            + jnp.dot(tri, lo, preferred_element_type=F32))
    a_cs_t = a_cs.T
    dt_t = dt.T
    a_last = a_cs[q - 1:q, :]
    lane = lax.broadcasted_iota(jnp.int32, (q, LANES), 1)
    first_head = lane < SSD_HEAD_DIM
    eacs_parts = _split3(jnp.exp(a_cs))
    wst_parts = _split3(jnp.exp(a_last - a_cs) * dt)

    def spread(parts, cols):
        return sum(jnp.dot(part, sel_pair_ref[:, cols], preferred_element_type=F32) for part in parts)

    for g in range(SSD_GROUPS):
        pair_cols = slice(g * SSD_GROUP_COLS, (g + 1) * SSD_GROUP_COLS)
        eacs_pairs = spread(eacs_parts, pair_cols)
        wst_pairs = spread(wst_parts, pair_cols)
        b_off = TOK_WIDTH + g * SSD_STATE
        c_off = TOK_WIDTH + SSD_BC_COLS + g * SSD_STATE
        bm = xc_ref[:, b_off:b_off + SSD_STATE]
        cm = xc_ref[:, c_off:c_off + SSD_STATE].astype(BF16)
        bm_t = bm.T.astype(BF16)
        cb = lax.dot_general(cm, bm.astype(BF16), (((1,), (1,)), ((), ())),
                             preferred_element_type=F32)
        sumsq = jnp.zeros((q, 1), F32)
        gated = []
        for pr in range(SSD_PAIRS_PER_GROUP):
            h0 = g * SSD_HEADS_PER_GROUP + 2 * pr
            xs = slice(g * SSD_GROUP_COLS + pr * LANES, g * SSD_GROUP_COLS + (pr + 1) * LANES)
            ss = slice(pr * LANES, (pr + 1) * LANES)
            xp = xc_ref[:, xs]
            xpb = xp.astype(BF16)
            res = []
            for h in (h0, h0 + 1):
                seg = a_cs[:, h:h + 1] - a_cs_t[h:h + 1, :]
                decay = jnp.exp(jnp.where(causal, seg, NEG))
                mat = (cb * decay * dt_t[h:h + 1, :]).astype(BF16)
                res.append(jnp.dot(mat, xpb, preferred_element_type=F32))
            y = jnp.where(first_head, res[0], res[1])
            s_prev = state_ref[g, :, ss]
            e_pair = eacs_pairs[:, ss]
            y = y + jnp.dot(cm, s_prev.astype(BF16), preferred_element_type=F32) * e_pair
            y = y + dskip_ref[:, xs] * xp
            xw = (xp * wst_pairs[:, ss]).astype(BF16)
            chunk_decay = e_pair[q - 1:q, :]
            state_ref[g, :, ss] = chunk_decay * s_prev + jnp.dot(bm_t, xw, preferred_element_type=F32)
            z = z_ref[:, xs].astype(F32)
            gz = y * _silu(z)
            sumsq = sumsq + jnp.sum(gz * gz, axis=-1, keepdims=True)
            gated.append(gz)
        rinv = lax.rsqrt(sumsq * (1.0 / SSD_GROUP_COLS) + EPS)
        for pr in range(SSD_PAIRS_PER_GROUP):
            xs = slice(g * SSD_GROUP_COLS + pr * LANES, g * SSD_GROUP_COLS + (pr + 1) * LANES)
            o_ref[:, xs] = ((gated[pr] * rinv) * ng_ref[:, xs]).astype(o_ref.dtype)


def _ssd_mixer(proj, dt_raw, conv_w, conv_b, dt_bias, a_log, d_skip, ssd_norm_g):
    b, t, _ = proj.shape
    q = SSD_CHUNK
    pad = LANES - SSD_HEADS
    full = lambda bi, ci: (0, 0)
    head_of_row = jnp.arange(LANES)[:, None]
    return pl.pallas_call(
        _ssd_kernel,
        grid=(b, t // q),
        in_specs=[pl.BlockSpec((None, q, SSD_CONV_DIM), lambda bi, ci: (bi, ci, 0)),
                  pl.BlockSpec((None, q, TOK_WIDTH), lambda bi, ci: (bi, ci, 2)),
                  pl.BlockSpec((None, q, LANES), lambda bi, ci: (bi, ci, 0)),
                  pl.BlockSpec((SSD_CONV, SSD_CONV_DIM), full),
                  pl.BlockSpec((1, SSD_CONV_DIM), full),
                  pl.BlockSpec((1, LANES), full),
                  pl.BlockSpec((1, LANES), full),
                  pl.BlockSpec((1, TOK_WIDTH), full),
                  pl.BlockSpec((1, TOK_WIDTH), full),
                  pl.BlockSpec((LANES, TOK_WIDTH), full)],
        out_specs=pl.BlockSpec((None, q, TOK_WIDTH), lambda bi, ci: (bi, ci, 0)),
        out_shape=jax.ShapeDtypeStruct((b, t, TOK_WIDTH), BF16),
        scratch_shapes=[pltpu.VMEM((HALO_ROWS, SSD_CONV_DIM), F32),
                        pltpu.VMEM((q, SSD_CONV_DIM), F32),
                        pltpu.VMEM((SSD_GROUPS, SSD_STATE, SSD_GROUP_COLS), F32)],
        compiler_params=_cparams("parallel", "arbitrary"),
        name="ssd_mixer",
    )(proj, proj, dt_raw, conv_w, conv_b.reshape(1, -1),
      jnp.pad(dt_bias, (0, pad)).reshape(1, LANES), jnp.pad(a_log, (0, pad)).reshape(1, LANES),
      jnp.repeat(d_skip, SSD_HEAD_DIM).reshape(1, TOK_WIDTH), ssd_norm_g.reshape(1, TOK_WIDTH),
      (head_of_row == jnp.arange(TOK_WIDTH)[None, :] // SSD_HEAD_DIM).astype(BF16))


ATTN_SLOTS = 4


def _dil_attn_kernel(slope_ref, q_ref, k_ref, v_ref, o_ref, lse_ref,
                     bias_ref, kt_ref, ktp_ref, vp_ref, s_ref, p_ref):
    c = ATTN_BLOCK
    n = pl.program_id(2)

    @pl.when((pl.program_id(0) == 0) & (pl.program_id(1) == 0) & (n == 0))
    def _():
        row = lax.broadcasted_iota(jnp.int32, (c, 2 * c), 0)
        col = lax.broadcasted_iota(jnp.int32, (c, 2 * c), 1)
        rel_pc = row + c - col
        rel_cp = jnp.where(col < c, row - col, row + 2 * c - col)
        band = lambda rel: (rel >= 0) & (rel <= c)
        for h in range(ATTN_HEADS):
            slope = slope_ref[h]
            bias_ref[0, h] = jnp.where(band(rel_pc), -slope * rel_pc.astype(F32), NEG)
            bias_ref[1, h] = jnp.where(band(rel_cp), -slope * rel_cp.astype(F32), NEG)
            bias_ref[2, h] = jnp.where(band(rel_cp) & (col < c), -slope * rel_cp.astype(F32), NEG)
        ktp_ref[...] = jnp.zeros_like(ktp_ref)
        vp_ref[...] = jnp.zeros_like(vp_ref)

    table_a = jnp.where(n == 0, 2, 1)
    lane = lax.broadcasted_iota(jnp.int32, (c, LANES), 1)
    lse_ref[...] = jnp.zeros_like(lse_ref)

    def head_cols(h):
        return slice(h * ATTN_HEAD_DIM, (h + 1) * ATTN_HEAD_DIM)

    def transpose_keys(blk, h):
        kt_ref[blk, head_cols(h), :] = k_ref[blk * c:(blk + 1) * c, head_cols(h)].T

    def scores(blk, h, slot):
        kt_a = kt_ref[0, head_cols(h), :]
        if blk == 0:
            kt_w = jnp.concatenate([kt_a, ktp_ref[head_cols(h), :]], axis=1)
            bias = bias_ref[table_a, h]
        else:
            kt_w = jnp.concatenate([kt_a, kt_ref[1, head_cols(h), :]], axis=1)
            bias = bias_ref[0, h]
        s = jnp.dot(q_ref[blk * c:(blk + 1) * c, head_cols(h)], kt_w, preferred_element_type=F32)
        s_ref[slot] = s + bias

    def softmax(blk, h, slot):
        s = s_ref[slot]
        m = jnp.max(s, axis=-1, keepdims=True)
        p = jnp.exp2(s - m)
        den = jnp.sum(p, axis=-1, keepdims=True)
        p_ref[slot] = p.astype(BF16)
        rows = slice(blk * c, (blk + 1) * c)
        lse_ref[rows, :] = jnp.where(lane == h, m, jnp.where(lane == ATTN_HEADS + h, den, lse_ref[rows, :]))

    def values(blk, h, slot):
        if blk == 0:
            vw = jnp.concatenate([v_ref[0:c, head_cols(h)], vp_ref[:, head_cols(h)]], axis=0)
        else:
            vw = v_ref[:, head_cols(h)]
        o_ref[blk * c:(blk + 1) * c, head_cols(h)] = jnp.dot(
            p_ref[slot], vw, preferred_element_type=F32).astype(o_ref.dtype)

    for blk in range(2):
        for h in range(ATTN_HEADS):
            transpose_keys(blk, h)

    items = [(blk, h) for blk in range(2) for h in range(ATTN_HEADS)]
    for i in range(len(items) + 2):
        if i < len(items):
            scores(*items[i], i % ATTN_SLOTS)
        if 1 <= i <= len(items):
            softmax(*items[i - 1], (i - 1) % ATTN_SLOTS)
        if i >= 2:
            values(*items[i - 2], (i - 2) % ATTN_SLOTS)

    ktp_ref[...] = kt_ref[1]
    vp_ref[...] = v_ref[c:2 * c, :]


def _dil_attn(qkv, slopes_d):
    b, d, n_sub, _ = qkv.shape
    c = ATTN_BLOCK
    assert n_sub % (2 * c) == 0
    blk = (None, None, 2 * c, TOK_WIDTH)
    col_block = lambda off: (lambda bi, r, n, *_: (bi, r, n, off))
    return pl.pallas_call(
        _dil_attn_kernel,
        grid_spec=pltpu.PrefetchScalarGridSpec(
            num_scalar_prefetch=1,
            grid=(b, d, n_sub // (2 * c)),
            in_specs=[pl.BlockSpec(blk, col_block(0)), pl.BlockSpec(blk, col_block(1)),
                      pl.BlockSpec(blk, col_block(2))],
            out_specs=[pl.BlockSpec(blk, col_block(0)),
                       pl.BlockSpec((None, None, 2 * c, LANES), col_block(0))],
            scratch_shapes=[pltpu.VMEM((3, ATTN_HEADS, c, 2 * c), F32),
                            pltpu.VMEM((2, TOK_WIDTH, c), BF16),
                            pltpu.VMEM((TOK_WIDTH, c), BF16),
                            pltpu.VMEM((c, TOK_WIDTH), BF16),
                            pltpu.VMEM((ATTN_SLOTS, c, 2 * c), F32),
                            pltpu.VMEM((ATTN_SLOTS, c, 2 * c), BF16)]),
        out_shape=[jax.ShapeDtypeStruct((b, d, n_sub, TOK_WIDTH), BF16),
                   jax.ShapeDtypeStruct((b, d, n_sub, LANES), F32)],
        compiler_params=_cparams("arbitrary", "arbitrary", "arbitrary"),
        name="dil_attn",
    )(slopes_d, qkv, qkv, qkv)


def _attn_mix_kernel(o0_ref, o1_ref, o2_ref, l0_ref, l1_ref, l2_ref, z_ref, out_ref, *, dils):
    tm = out_ref.shape[0]
    row = lax.broadcasted_iota(jnp.int32, (tm, tm), 0)
    col = lax.broadcasted_iota(jnp.int32, (tm, tm), 1)

    def perm(d):
        sub = tm // d
        return jnp.where((row % d) * sub + row // d == col, 1.0, 0.0).astype(BF16)

    def stacked(ref, d):
        return ref[0] if d == 1 else jnp.concatenate([ref[r] for r in range(d)], axis=0)

    perms = [None if d == 1 else perm(d) for d in dils]
    o_refs = (o0_ref, o1_ref, o2_ref)
    maxes, dens = [], []
    for ref, d, pm in zip((l0_ref, l1_ref, l2_ref), dils, perms):
        st = stacked(ref, d)
        if pm is not None:
            st = sum(jnp.dot(pm, part, preferred_element_type=F32) for part in _split3(st))
        maxes.append(st)
        dens.append(pltpu.roll(st, LANES - ATTN_HEADS, axis=1))
    mx = jnp.maximum(jnp.maximum(maxes[0], maxes[1]), maxes[2])
    es = [jnp.exp2(m - mx) for m in maxes]
    inv = 1.0 / (es[0] * dens[0] + es[1] * dens[1] + es[2] * dens[2])
    ws = [e * inv for e in es]
    for h in range(ATTN_HEADS):
        sl = slice(h * ATTN_HEAD_DIM, (h + 1) * ATTN_HEAD_DIM)
        y = None
        for ref, d, pm, w in zip(o_refs, dils, perms, ws):
            if d == 1:
                o = ref[0, :, sl].astype(F32)
            else:
                o = jnp.dot(pm, jnp.concatenate([ref[r, :, sl] for r in range(d)], axis=0),
                            preferred_element_type=F32)
            term = jnp.broadcast_to(w[:, h:h + 1], (tm, ATTN_HEAD_DIM)) * o
            y = term if y is None else y + term
        out_ref[:, sl] = (y * _silu(z_ref[:, sl].astype(F32))).astype(out_ref.dtype)


def _attn_mix(os_, ls_, rest, *, tm):
    b, t, _ = rest.shape
    tm = min(tm, t)
    dils = tuple(o.shape[1] for o in os_)
    grp = lambda d, w: pl.BlockSpec((None, d, tm // d, w), lambda bi, i: (bi, 0, i, 0))
    tok = pl.BlockSpec((None, tm, TOK_WIDTH), lambda bi, i: (bi, i, 0))
    return pl.pallas_call(
        functools.partial(_attn_mix_kernel, dils=dils),
        grid=(b, t // tm),
        in_specs=[grp(d, TOK_WIDTH) for d in dils] + [grp(d, LANES) for d in dils] + [tok],
        out_specs=tok,
        out_shape=jax.ShapeDtypeStruct((b, t, TOK_WIDTH), BF16),
        compiler_params=_cparams("parallel", "parallel"),
        name="attn_mix",
    )(*os_, *ls_, rest)


def _mem_kv(mem, mem_norm_g, w_mem_kv):
    b, n, dm = mem.shape
    (hm,) = _norm(mem.reshape(1, b * n, dm), mem_norm_g)
    kv = _proj(hm, w_mem_kv.astype(BF16), tn=1024, out_dtype=BF16)
    return kv.reshape(b, n, 2 * MEM_WIDTH)


def _finish_layer(x2, gated_tok, gated_mem, w_out, next_g, mode):
    m = x2.shape[0]
    wt = w_out[:TOK_WIDTH].astype(BF16)
    wm = w_out[TOK_WIDTH:].astype(BF16)
    return _out_proj(gated_tok.reshape(m, TOK_WIDTH), gated_mem.reshape(m, MEM_WIDTH), wt, wm, x2,
                     next_g, mode=mode)


def _ssd_layer(x2, h, bt, kv, norm_g, w_in, conv_w, conv_b, dt_bias, a_log, d_skip, ssd_norm_g, w_out,
               next_g=None, mode="plain"):
    b, t = bt
    o_dt = SSD_CONV_DIM
    o_q = o_dt + SSD_HEADS
    o_z = o_q + MEM_WIDTH
    w_main = jnp.concatenate([w_in[:, :o_dt], w_in[:, o_q:o_z] * MEM_Q_SCALE, w_in[:, o_z:]], axis=1).astype(BF16)
    w_dt = jnp.pad(w_in[:, o_dt:o_q], ((0, 0), (0, LANES - SSD_HEADS))).astype(BF16)
    if h is None:
        (h,) = _norm(x2.reshape(b, t, -1), norm_g)
    proj = _proj(h, w_main, tn=2048, out_dtype=BF16).reshape(b, t, -1)
    dt_raw = _proj(h, w_dt, tn=LANES, out_dtype=F32).reshape(b, t, LANES)
    gated_tok = _ssd_mixer(proj, dt_raw, conv_w, conv_b, dt_bias, a_log, d_skip, ssd_norm_g)
    gated_mem = _mem_attn(proj, kv, q_block=SSD_CONV_DIM // MEM_WIDTH,
                          z_block=(SSD_CONV_DIM + MEM_WIDTH + TOK_WIDTH) // MEM_WIDTH, tq=512)
    return _finish_layer(x2, gated_tok, gated_mem, w_out, next_g, mode)


def _attn_layer(x2, bt, kv, norm_g, w_in, w_out, next_g=None, mode="plain"):
    b, t = bt
    o_q = N_DIL * ATTN_GROUP_COLS
    o_z = o_q + MEM_WIDTH
    x3 = x2.reshape(b, t, -1)
    w_rest = jnp.concatenate([w_in[:, o_z:o_z + TOK_WIDTH], w_in[:, o_q:o_z] * MEM_Q_SCALE,
                              w_in[:, o_z + TOK_WIDTH:]], axis=1).astype(BF16)
    dils = tuple(d for _, d in DILATED_GROUPS)
    hs = dict(zip(dils, _norm(x3, norm_g, dils)))
    rest = _proj(hs[1], w_rest, tn=2560, out_dtype=BF16).reshape(b, t, -1)
    slopes = jnp.exp2(-ALIBI_MAX_EXP * jnp.arange(1, N_DIL * ATTN_HEADS + 1, dtype=F32) / (N_DIL * ATTN_HEADS))
    slopes = slopes.reshape(N_DIL, ATTN_HEADS)
    q_scale = (ATTN_HEAD_DIM ** -0.5) * LOG2E
    outs, lses = [], []
    col = jnp.arange(w_in.shape[1])
    is_q = (col < o_q) & (col % ATTN_GROUP_COLS < TOK_WIDTH)
    w_all = (w_in * jnp.where(is_q, q_scale, 1.0).astype(F32)).astype(BF16)
    for g, (window, d) in enumerate(DILATED_GROUPS):
        assert window // d == ATTN_BLOCK
        qkv = _proj(hs[d], w_all, tn=2304, out_dtype=BF16, dil=d,
                    col0=g * ATTN_GROUP_COLS, ncols=ATTN_GROUP_COLS)
        o, lse = _dil_attn(qkv, slopes[g] * (float(d) * LOG2E))
        outs.append(o)
        lses.append(lse)
    gated_tok = _attn_mix(outs, lses, rest, tm=256)
    gated_mem = _mem_attn(rest, kv, q_block=TOK_WIDTH // MEM_WIDTH,
                          z_block=(TOK_WIDTH + MEM_WIDTH) // MEM_WIDTH, tq=512)
    return _finish_layer(x2, gated_tok, gated_mem, w_out, next_g, mode)


def kernel(x, mem, mem_norm_g, final_norm_g, norm_g_0, w_in_0, conv_w_0, conv_b_0, dt_bias_0, a_log_0, d_skip_0, ssd_norm_g_0, w_mem_kv_0, w_out_0, norm_g_1, w_in_1, w_mem_kv_1, w_out_1, norm_g_2, w_in_2, conv_w_2, conv_b_2, dt_bias_2, a_log_2, d_skip_2, ssd_norm_g_2, w_mem_kv_2, w_out_2, norm_g_3, w_in_3, w_mem_kv_3, w_out_3):
    b, t, dm = x.shape
    bt = (b, t)
    x2 = x.reshape(b * t, dm)
    (x2,) = _ssd_layer(x2, None, bt, _mem_kv(mem, mem_norm_g, w_mem_kv_0), norm_g_0, w_in_0, conv_w_0, conv_b_0,
                       dt_bias_0, a_log_0, d_skip_0, ssd_norm_g_0, w_out_0)
    x2, h2 = _attn_layer(x2, bt, _mem_kv(mem, mem_norm_g, w_mem_kv_1), norm_g_1, w_in_1, w_out_1,
                         next_g=norm_g_2, mode="norm")
    (x2,) = _ssd_layer(x2, h2.reshape(b, t, dm), bt, _mem_kv(mem, mem_norm_g, w_mem_kv_2), norm_g_2, w_in_2,
                       conv_w_2, conv_b_2, dt_bias_2, a_log_2, d_skip_2, ssd_norm_g_2, w_out_2)
    (y,) = _attn_layer(x2, bt, _mem_kv(mem, mem_norm_g, w_mem_kv_3), norm_g_3, w_in_3, w_out_3,
                       next_g=final_norm_g, mode="final")
    return y.reshape(b, t, dm)
```

```python
import functools

import jax
import jax.numpy as jnp
from jax import lax
from jax.experimental import pallas as pl
from jax.experimental.pallas import tpu as pltpu

F32 = jnp.float32
BF16 = jnp.bfloat16
EPS = 1e-6
NEG = -1e30
LOG2E = 1.4426950408889634

LANES = 128
D_MODEL = 2048
N_MEM = 256
MIX_WIDTH = 2 * D_MODEL
MEM_WIDTH = MIX_WIDTH // 4
TOK_WIDTH = MIX_WIDTH - MEM_WIDTH
MEM_HEADS = 4
MEM_HEAD_DIM = MEM_WIDTH // MEM_HEADS

SSD_HEAD_DIM = 64
SSD_HEADS = TOK_WIDTH // SSD_HEAD_DIM
SSD_GROUPS = 8
SSD_HEADS_PER_GROUP = SSD_HEADS // SSD_GROUPS
SSD_GROUP_COLS = SSD_HEADS_PER_GROUP * SSD_HEAD_DIM
SSD_PAIRS_PER_GROUP = SSD_GROUP_COLS // LANES
SSD_STATE = 128
SSD_CONV = 4
SSD_CHUNK = 128
SSD_BC_COLS = SSD_GROUPS * SSD_STATE
SSD_CONV_DIM = TOK_WIDTH + 2 * SSD_BC_COLS
HALO_ROWS = 8

ATTN_HEAD_DIM = 128
ATTN_HEADS = TOK_WIDTH // ATTN_HEAD_DIM
DILATED_GROUPS = ((128, 1), (512, 4), (2048, 16))
N_DIL = len(DILATED_GROUPS)
ATTN_GROUP_COLS = 3 * TOK_WIDTH
ATTN_BLOCK = 128
ALIBI_MAX_EXP = 8.0

VMEM_LIMIT = 56 * 1024 * 1024


def _cparams(*sem):
    return pltpu.CompilerParams(dimension_semantics=sem, vmem_limit_bytes=VMEM_LIMIT)


def _silu(v):
    hv = 0.5 * v
    return hv + hv * jnp.tanh(hv)


def _split3(v):
    hi = v.astype(BF16)
    r1 = v - hi.astype(F32)
    mid = r1.astype(BF16)
    lo = (r1 - mid.astype(F32)).astype(BF16)
    return hi, mid, lo


ROW_TILE = 1024
PERM_CHUNK = 256
NORM_ROWS = 32


def _norm_kernel(x_ref, g_ref, *out_refs, dils):
    tm = x_ref.shape[0]
    nat_ref = out_refs[0]

    def body(i, carry):
        r = pl.multiple_of(i * NORM_ROWS, NORM_ROWS)
        xf = x_ref[pl.ds(r, NORM_ROWS), :]
        ms = jnp.mean(xf * xf, axis=-1, keepdims=True)
        nat_ref[pl.ds(r, NORM_ROWS), :] = ((xf * lax.rsqrt(ms + EPS)) * g_ref[...]).astype(BF16)
        return carry
    lax.fori_loop(0, tm // NORM_ROWS, body, 0)

    row = lax.broadcasted_iota(jnp.int32, (PERM_CHUNK, PERM_CHUNK), 0)
    col = lax.broadcasted_iota(jnp.int32, (PERM_CHUNK, PERM_CHUNK), 1)
    for d, o_ref in zip(dils[1:], out_refs[1:]):
        sub = tm // d
        csub = PERM_CHUNK // d
        perm = jnp.where((row % csub) * d + row // csub == col, 1.0, 0.0).astype(BF16)
        for c in range(tm // PERM_CHUNK):
            hp = jnp.dot(perm, nat_ref[c * PERM_CHUNK:(c + 1) * PERM_CHUNK, :],
                         preferred_element_type=F32).astype(BF16)
            for r in range(d):
                o_ref[r * sub + c * csub:r * sub + (c + 1) * csub, :] = hp[r * csub:(r + 1) * csub, :]


def _norm(x, g, dils=(1,)):
    b, t, k = x.shape
    tm = min(ROW_TILE, t)
    assert dils[0] == 1 and t % tm == 0 and tm % PERM_CHUNK == 0
    blk = pl.BlockSpec((None, tm, k), lambda bi, i: (bi, i, 0))
    return pl.pallas_call(
        functools.partial(_norm_kernel, dils=dils),
        grid=(b, t // tm),
        in_specs=[blk, pl.BlockSpec((1, k), lambda bi, i: (0, 0))],
        out_specs=[blk] * len(dils),
        out_shape=[jax.ShapeDtypeStruct((b, t, k), BF16)] * len(dils),
        compiler_params=_cparams("parallel", "parallel"),
        name="pre_norm",
    )(x, g.reshape(1, k))


def _proj_kernel(h_ref, w_ref, o_ref):
    res = jnp.dot(h_ref[...], w_ref[...], preferred_element_type=F32).astype(o_ref.dtype)
    o_ref[...] = res.reshape(o_ref.shape)


def _proj(h, w, *, tn, out_dtype, dil=1, col0=0, ncols=None):
    b, t, k = h.shape
    n = w.shape[1] if ncols is None else ncols
    tm = min(ROW_TILE, t)
    tn = min(tn, n)
    assert t % tm == 0 and n % tn == 0 and col0 % tn == 0 and tm % dil == 0
    j0 = col0 // tn
    return pl.pallas_call(
        _proj_kernel,
        grid=(b, t // tm, n // tn),
        in_specs=[pl.BlockSpec((None, tm, k), lambda bi, i, j: (bi, i, 0)),
                  pl.BlockSpec((k, tn), lambda bi, i, j: (0, j0 + j))],
        out_specs=pl.BlockSpec((None, dil, tm // dil, tn), lambda bi, i, j: (bi, 0, i, j)),
        out_shape=jax.ShapeDtypeStruct((b, dil, t // dil, n), out_dtype),
        compiler_params=_cparams("parallel", "parallel", "parallel"),
        name="proj",
    )(h, w)


OUT_ROWS = 512


def _out_proj_kernel(gt_ref, gm_ref, wt_ref, wm_ref, x_ref, *rest, mode):
    g_ref = None if mode == "plain" else rest[0]
    new_ref = rest[-1] if mode != "norm" else rest[-2]
    acc = jnp.dot(gt_ref[...], wt_ref[...], preferred_element_type=F32)
    acc = acc + jnp.dot(gm_ref[...], wm_ref[...], preferred_element_type=F32)
    new_ref[...] = x_ref[...] + acc
    if mode == "plain":
        return
    normed_ref = rest[-1] if mode == "norm" else rest[-2]

    def body(i, carry):
        r = pl.multiple_of(i * NORM_ROWS, NORM_ROWS)
        xf = new_ref[pl.ds(r, NORM_ROWS), :]
        ms = jnp.mean(xf * xf, axis=-1, keepdims=True)
        hn = (xf * lax.rsqrt(ms + EPS)) * g_ref[...]
        normed_ref[pl.ds(r, NORM_ROWS), :] = hn.astype(normed_ref.dtype)
        return carry
    lax.fori_loop(0, new_ref.shape[0] // NORM_ROWS, body, 0, unroll=2)


def _out_proj(gt, gm, wt, wm, x, g=None, *, mode):
    m, n = x.shape
    tm = min(OUT_ROWS, m)
    assert m % tm == 0 and tm % NORM_ROWS == 0 and (g is None) == (mode == "plain")
    rows = lambda width: pl.BlockSpec((tm, width), lambda i: (i, 0))
    resident = lambda arr: pl.BlockSpec(arr.shape, lambda i: (0, 0), pipeline_mode=pl.Buffered(1))
    in_specs = [rows(gt.shape[1]), rows(gm.shape[1]), resident(wt), resident(wm), rows(n)]
    args = [gt, gm, wt, wm, x]
    if g is not None:
        in_specs.append(pl.BlockSpec((1, n), lambda i: (0, 0)))
        args.append(g.reshape(1, n))
    out_shape = [jax.ShapeDtypeStruct((m, n), F32)]
    if mode == "norm":
        out_shape.append(jax.ShapeDtypeStruct((m, n), BF16))
    return pl.pallas_call(
        functools.partial(_out_proj_kernel, mode=mode),
        grid=(m // tm,),
        in_specs=in_specs,
        out_specs=[rows(n)] * len(out_shape),
        out_shape=out_shape,
        scratch_shapes=[pltpu.VMEM((tm, n), F32)] if mode == "final" else [],
        compiler_params=_cparams("parallel"),
        name="out_proj",
    )(*args)


MEM_Q_SCALE = (MEM_HEAD_DIM ** -0.5) * LOG2E


def _mem_attn_kernel(q_ref, z_ref, mk_ref, mv_ref, o_ref):
    for h in range(MEM_HEADS):
        sl = slice(h * MEM_HEAD_DIM, (h + 1) * MEM_HEAD_DIM)
        s = lax.dot_general(q_ref[:, sl], mk_ref[:, sl], (((1,), (1,)), ((), ())),
                            preferred_element_type=F32)
        m = jnp.max(s, axis=-1, keepdims=True)
        p = jnp.exp2(s - m)
        den = jnp.sum(p, axis=-1, keepdims=True)
        o = jnp.dot(p.astype(BF16), mv_ref[:, sl], preferred_element_type=F32) / den
        z = z_ref[:, sl].astype(F32)
        o_ref[:, sl] = (o * _silu(z)).astype(o_ref.dtype)


def _mem_attn(proj, kv, *, q_block, z_block, tq):
    b, t, _ = proj.shape
    tq = min(tq, t)
    return pl.pallas_call(
        _mem_attn_kernel,
        grid=(b, t // tq),
        in_specs=[pl.BlockSpec((None, tq, MEM_WIDTH), lambda bi, i: (bi, i, q_block)),
                  pl.BlockSpec((None, tq, MEM_WIDTH), lambda bi, i: (bi, i, z_block)),
                  pl.BlockSpec((None, N_MEM, MEM_WIDTH), lambda bi, i: (bi, 0, 0)),
                  pl.BlockSpec((None, N_MEM, MEM_WIDTH), lambda bi, i: (bi, 0, 1))],
        out_specs=pl.BlockSpec((None, tq, MEM_WIDTH), lambda bi, i: (bi, i, 0)),
        out_shape=jax.ShapeDtypeStruct((b, t, MEM_WIDTH), BF16),
        compiler_params=_cparams("parallel", "parallel"),
        name="mem_attn",
    )(proj, proj, kv, kv)


def _ssd_kernel(xbc_ref, z_ref, dt_ref, cw_ref, cb_ref, dtb_ref, alog_ref, dskip_ref, ng_ref,
                sel_pair_ref, o_ref, halo_ref, xc_ref, state_ref):
    q = SSD_CHUNK
    c = pl.program_id(1)

    @pl.when(c == 0)
    def _():
        halo_ref[...] = jnp.zeros_like(halo_ref)
        state_ref[...] = jnp.zeros_like(state_ref)

    for t in range(SSD_CONV_DIM // LANES):
        cs = slice(t * LANES, (t + 1) * LANES)
        u = xbc_ref[:, cs].astype(F32)
        ext = jnp.concatenate([halo_ref[:, cs], u], axis=0)
        acc = cb_ref[:, cs] + cw_ref[SSD_CONV - 1:SSD_CONV, cs] * u
        for k in range(SSD_CONV - 1):
            shifted = pltpu.roll(ext, SSD_CONV - 1 - k, axis=0)[HALO_ROWS:, :]
            acc = acc + cw_ref[k:k + 1, cs] * shifted
        xc_ref[:, cs] = _silu(acc)
        halo_ref[:, cs] = u[q - HALO_ROWS:, :]

    dtr = dt_ref[...] + dtb_ref[...]
    dt = jnp.maximum(dtr, 0.0) + jnp.log1p(jnp.exp(-jnp.abs(dtr)))
    a = -jnp.exp(alog_ref[...])
    dta = dt * a
    row = lax.broadcasted_iota(jnp.int32, (q, q), 0)
    col = lax.broadcasted_iota(jnp.int32, (q, q), 1)
    causal = row >= col
    tri = jnp.where(causal, 1.0, 0.0).astype(BF16)
    hi, mid, lo = _split3(dta)
    a_cs = (jnp.dot(tri, hi, preferred_element_type=F32)
            + jnp.dot(tri, mid, preferred_element_type=F32)
            + jnp.dot(tri, lo, preferred_element_type=F32))
    a_cs_t = a_cs.T
    dt_t = dt.T
    a_last = a_cs[q - 1:q, :]
    lane = lax.broadcasted_iota(jnp.int32, (q, LANES), 1)
    first_head = lane < SSD_HEAD_DIM
    eacs_parts = _split3(jnp.exp(a_cs))
    wst_parts = _split3(jnp.exp(a_last - a_cs) * dt)

    def spread(parts, cols):
        return sum(jnp.dot(part, sel_pair_ref[:, cols], preferred_element_type=F32) for part in parts)

    for g in range(SSD_GROUPS):
        pair_cols = slice(g * SSD_GROUP_COLS, (g + 1) * SSD_GROUP_COLS)
        eacs_pairs = spread(eacs_parts, pair_cols)
        wst_pairs = spread(wst_parts, pair_cols)
        b_off = TOK_WIDTH + g * SSD_STATE
        c_off = TOK_WIDTH + SSD_BC_COLS + g * SSD_STATE
        bm = xc_ref[:, b_off:b_off + SSD_STATE]
        cm = xc_ref[:, c_off:c_off + SSD_STATE].astype(BF16)
        bm_t = bm.T.astype(BF16)
        cb = lax.dot_general(cm, bm.astype(BF16), (((1,), (1,)), ((), ())),
                             preferred_element_type=F32)
        sumsq = jnp.zeros((q, 1), F32)
        gated = []
        for pr in range(SSD_PAIRS_PER_GROUP):
            h0 = g * SSD_HEADS_PER_GROUP + 2 * pr
            xs = slice(g * SSD_GROUP_COLS + pr * LANES, g * SSD_GROUP_COLS + (pr + 1) * LANES)
            ss = slice(pr * LANES, (pr + 1) * LANES)
            xp = xc_ref[:, xs]
            xpb = xp.astype(BF16)
            res = []
            for h in (h0, h0 + 1):
                seg = a_cs[:, h:h + 1] - a_cs_t[h:h + 1, :]
                decay = jnp.exp(jnp.where(causal, seg, NEG))
                mat = (cb * decay * dt_t[h:h + 1, :]).astype(BF16)
                res.append(jnp.dot(mat, xpb, preferred_element_type=F32))
            y = jnp.where(first_head, res[0], res[1])
            s_prev = state_ref[g, :, ss]
            e_pair = eacs_pairs[:, ss]
            y = y + jnp.dot(cm, s_prev.astype(BF16), preferred_element_type=F32) * e_pair
            y = y + dskip_ref[:, xs] * xp
            xw = (xp * wst_pairs[:, ss]).astype(BF16)
            chunk_decay = e_pair[q - 1:q, :]
            state_ref[g, :, ss] = chunk_decay * s_prev + jnp.dot(bm_t, xw, preferred_element_type=F32)
            z = z_ref[:, xs].astype(F32)
            gz = y * _silu(z)
            sumsq = sumsq + jnp.sum(gz * gz, axis=-1, keepdims=True)
            gated.append(gz)
        rinv = lax.rsqrt(sumsq * (1.0 / SSD_GROUP_COLS) + EPS)
        for pr in range(SSD_PAIRS_PER_GROUP):
            xs = slice(g * SSD_GROUP_COLS + pr * LANES, g * SSD_GROUP_COLS + (pr + 1) * LANES)
            o_ref[:, xs] = ((gated[pr] * rinv) * ng_ref[:, xs]).astype(o_ref.dtype)


def _ssd_mixer(proj, dt_raw, conv_w, conv_b, dt_bias, a_log, d_skip, ssd_norm_g):
    b, t, _ = proj.shape
    q = SSD_CHUNK
    pad = LANES - SSD_HEADS
    full = lambda bi, ci: (0, 0)
    head_of_row = jnp.arange(LANES)[:, None]
    return pl.pallas_call(
        _ssd_kernel,
        grid=(b, t // q),
        in_specs=[pl.BlockSpec((None, q, SSD_CONV_DIM), lambda bi, ci: (bi, ci, 0)),
                  pl.BlockSpec((None, q, TOK_WIDTH), lambda bi, ci: (bi, ci, 2)),
                  pl.BlockSpec((None, q, LANES), lambda bi, ci: (bi, ci, 0)),
                  pl.BlockSpec((SSD_CONV, SSD_CONV_DIM), full),
                  pl.BlockSpec((1, SSD_CONV_DIM), full),
                  pl.BlockSpec((1, LANES), full),
                  pl.BlockSpec((1, LANES), full),
                  pl.BlockSpec((1, TOK_WIDTH), full),
                  pl.BlockSpec((1, TOK_WIDTH), full),
                  pl.BlockSpec((LANES, TOK_WIDTH), full)],
        out_specs=pl.BlockSpec((None, q, TOK_WIDTH), lambda bi, ci: (bi, ci, 0)),
        out_shape=jax.ShapeDtypeStruct((b, t, TOK_WIDTH), BF16),
        scratch_shapes=[pltpu.VMEM((HALO_ROWS, SSD_CONV_DIM), F32),
                        pltpu.VMEM((q, SSD_CONV_DIM), F32),
                        pltpu.VMEM((SSD_GROUPS, SSD_STATE, SSD_GROUP_COLS), F32)],
        compiler_params=_cparams("parallel", "arbitrary"),
        name="ssd_mixer",
    )(proj, proj, dt_raw, conv_w, conv_b.reshape(1, -1),
      jnp.pad(dt_bias, (0, pad)).reshape(1, LANES), jnp.pad(a_log, (0, pad)).reshape(1, LANES),
      jnp.repeat(d_skip, SSD_HEAD_DIM).reshape(1, TOK_WIDTH), ssd_norm_g.reshape(1, TOK_WIDTH),
      (head_of_row == jnp.arange(TOK_WIDTH)[None, :] // SSD_HEAD_DIM).astype(BF16))


ATTN_SLOTS = 4


def _dil_attn_kernel(slope_ref, q_ref, k_ref, v_ref, o_ref, lse_ref,
                     bias_ref, kt_ref, ktp_ref, vp_ref, s_ref, p_ref):
    c = ATTN_BLOCK
    n = pl.program_id(2)

    @pl.when((pl.program_id(0) == 0) & (pl.program_id(1) == 0) & (n == 0))
    def _():
        row = lax.broadcasted_iota(jnp.int32, (c, 2 * c), 0)
        col = lax.broadcasted_iota(jnp.int32, (c, 2 * c), 1)
        rel_pc = row + c - col
        rel_cp = jnp.where(col < c, row - col, row + 2 * c - col)
        band = lambda rel: (rel >= 0) & (rel <= c)
        for h in range(ATTN_HEADS):
            slope = slope_ref[h]
            bias_ref[0, h] = jnp.where(band(rel_pc), -slope * rel_pc.astype(F32), NEG)
            bias_ref[1, h] = jnp.where(band(rel_cp), -slope * rel_cp.astype(F32), NEG)
            bias_ref[2, h] = jnp.where(band(rel_cp) & (col < c), -slope * rel_cp.astype(F32), NEG)
        ktp_ref[...] = jnp.zeros_like(ktp_ref)
        vp_ref[...] = jnp.zeros_like(vp_ref)

    table_a = jnp.where(n == 0, 2, 1)
    lane = lax.broadcasted_iota(jnp.int32, (c, LANES), 1)
    lse_ref[...] = jnp.zeros_like(lse_ref)

    def head_cols(h):
        return slice(h * ATTN_HEAD_DIM, (h + 1) * ATTN_HEAD_DIM)

    def transpose_keys(blk, h):
        kt_ref[blk, head_cols(h), :] = k_ref[blk * c:(blk + 1) * c, head_cols(h)].T

    def scores(blk, h, slot):
        kt_a = kt_ref[0, head_cols(h), :]
        if blk == 0:
            kt_w = jnp.concatenate([kt_a, ktp_ref[head_cols(h), :]], axis=1)
            bias = bias_ref[table_a, h]
        else:
            kt_w = jnp.concatenate([kt_a, kt_ref[1, head_cols(h), :]], axis=1)
            bias = bias_ref[0, h]
        s = jnp.dot(q_ref[blk * c:(blk + 1) * c, head_cols(h)], kt_w, preferred_element_type=F32)
        s_ref[slot] = s + bias

    def softmax(blk, h, slot):
        s = s_ref[slot]
        m = jnp.max(s, axis=-1, keepdims=True)
        p = jnp.exp2(s - m)
        den = jnp.sum(p, axis=-1, keepdims=True)
        p_ref[slot] = p.astype(BF16)
        rows = slice(blk * c, (blk + 1) * c)
        lse_ref[rows, :] = jnp.where(lane == h, m, jnp.where(lane == ATTN_HEADS + h, den, lse_ref[rows, :]))

    def values(blk, h, slot):
        if blk == 0:
            vw = jnp.concatenate([v_ref[0:c, head_cols(h)], vp_ref[:, head_cols(h)]], axis=0)
        else:
            vw = v_ref[:, head_cols(h)]
        o_ref[blk * c:(blk + 1) * c, head_cols(h)] = jnp.dot(
            p_ref[slot], vw, preferred_element_type=F32).astype(o_ref.dtype)

    for blk in range(2):
        for h in range(ATTN_HEADS):
            transpose_keys(blk, h)

    items = [(blk, h) for blk in range(2) for h in range(ATTN_HEADS)]
    for i in range(len(items) + 2):
        if i < len(items):
            scores(*items[i], i % ATTN_SLOTS)
        if 1 <= i <= len(items):
            softmax(*items[i - 1], (i - 1) % ATTN_SLOTS)
        if i >= 2:
            values(*items[i - 2], (i - 2) % ATTN_SLOTS)

    ktp_ref[...] = kt_ref[1]
    vp_ref[...] = v_ref[c:2 * c, :]


def _dil_attn(qkv, slopes_d):
    b, d, n_sub, _ = qkv.shape
    c = ATTN_BLOCK
    assert n_sub % (2 * c) == 0
    blk = (None, None, 2 * c, TOK_WIDTH)
    col_block = lambda off: (lambda bi, r, n, *_: (bi, r, n, off))
    return pl.pallas_call(
        _dil_attn_kernel,
        grid_spec=pltpu.PrefetchScalarGridSpec(
            num_scalar_prefetch=1,
            grid=(b, d, n_sub // (2 * c)),
            in_specs=[pl.BlockSpec(blk, col_block(0)), pl.BlockSpec(blk, col_block(1)),
                      pl.BlockSpec(blk, col_block(2))],
            out_specs=[pl.BlockSpec(blk, col_block(0)),
                       pl.BlockSpec((None, None, 2 * c, LANES), col_block(0))],
            scratch_shapes=[pltpu.VMEM((3, ATTN_HEADS, c, 2 * c), F32),
                            pltpu.VMEM((2, TOK_WIDTH, c), BF16),
                            pltpu.VMEM((TOK_WIDTH, c), BF16),
                            pltpu.VMEM((c, TOK_WIDTH), BF16),
                            pltpu.VMEM((ATTN_SLOTS, c, 2 * c), F32),
                            pltpu.VMEM((ATTN_SLOTS, c, 2 * c), BF16)]),
        out_shape=[jax.ShapeDtypeStruct((b, d, n_sub, TOK_WIDTH), BF16),
                   jax.ShapeDtypeStruct((b, d, n_sub, LANES), F32)],
        compiler_params=_cparams("arbitrary", "arbitrary", "arbitrary"),
        name="dil_attn",
    )(slopes_d, qkv, qkv, qkv)


def _attn_mix_kernel(o0_ref, o1_ref, o2_ref, l0_ref, l1_ref, l2_ref, z_ref, out_ref, *, dils):
    tm = out_ref.shape[0]
    row = lax.broadcasted_iota(jnp.int32, (tm, tm), 0)
    col = lax.broadcasted_iota(jnp.int32, (tm, tm), 1)

    def perm(d):
        sub = tm // d
        return jnp.where((row % d) * sub + row // d == col, 1.0, 0.0).astype(BF16)

    def stacked(ref, d):
        return ref[0] if d == 1 else jnp.concatenate([ref[r] for r in range(d)], axis=0)

    perms = [None if d == 1 else perm(d) for d in dils]
    o_refs = (o0_ref, o1_ref, o2_ref)
    maxes, dens = [], []
    for ref, d, pm in zip((l0_ref, l1_ref, l2_ref), dils, perms):
        st = stacked(ref, d)
        if pm is not None:
            st = sum(jnp.dot(pm, part, preferred_element_type=F32) for part in _split3(st))
        maxes.append(st)
        dens.append(pltpu.roll(st, LANES - ATTN_HEADS, axis=1))
    mx = jnp.maximum(jnp.maximum(maxes[0], maxes[1]), maxes[2])
    es = [jnp.exp2(m - mx) for m in maxes]
    inv = 1.0 / (es[0] * dens[0] + es[1] * dens[1] + es[2] * dens[2])
    ws = [e * inv for e in es]
    for h in range(ATTN_HEADS):
        sl = slice(h * ATTN_HEAD_DIM, (h + 1) * ATTN_HEAD_DIM)
        y = None
        for ref, d, pm, w in zip(o_refs, dils, perms, ws):
            if d == 1:
                o = ref[0, :, sl].astype(F32)
            else:
                o = jnp.dot(pm, jnp.concatenate([ref[r, :, sl] for r in range(d)], axis=0),
                            preferred_element_type=F32)
            term = jnp.broadcast_to(w[:, h:h + 1], (tm, ATTN_HEAD_DIM)) * o
            y = term if y is None else y + term
        out_ref[:, sl] = (y * _silu(z_ref[:, sl].astype(F32))).astype(out_ref.dtype)


def _attn_mix(os_, ls_, rest, *, tm):
    b, t, _ = rest.shape
    tm = min(tm, t)
    dils = tuple(o.shape[1] for o in os_)
    grp = lambda d, w: pl.BlockSpec((None, d, tm // d, w), lambda bi, i: (bi, 0, i, 0))
    tok = pl.BlockSpec((None, tm, TOK_WIDTH), lambda bi, i: (bi, i, 0))
    return pl.pallas_call(
        functools.partial(_attn_mix_kernel, dils=dils),
        grid=(b, t // tm),
        in_specs=[grp(d, TOK_WIDTH) for d in dils] + [grp(d, LANES) for d in dils] + [tok],
        out_specs=tok,
        out_shape=jax.ShapeDtypeStruct((b, t, TOK_WIDTH), BF16),
        compiler_params=_cparams("parallel", "parallel"),
        name="attn_mix",
    )(*os_, *ls_, rest)


def _mem_kv(mem, mem_norm_g, w_mem_kv):
    b, n, dm = mem.shape
    (hm,) = _norm(mem.reshape(1, b * n, dm), mem_norm_g)
    kv = _proj(hm, w_mem_kv.astype(BF16), tn=1024, out_dtype=BF16)
    return kv.reshape(b, n, 2 * MEM_WIDTH)


def _finish_layer(x2, gated_tok, gated_mem, w_out, next_g, mode):
    m = x2.shape[0]
    wt = w_out[:TOK_WIDTH].astype(BF16)
    wm = w_out[TOK_WIDTH:].astype(BF16)
    return _out_proj(gated_tok.reshape(m, TOK_WIDTH), gated_mem.reshape(m, MEM_WIDTH), wt, wm, x2,
                     next_g, mode=mode)


def _ssd_layer(x2, h, bt, kv, norm_g, w_in, conv_w, conv_b, dt_bias, a_log, d_skip, ssd_norm_g, w_out,
               next_g=None, mode="plain"):
    b, t = bt
    o_dt = SSD_CONV_DIM
    o_q = o_dt + SSD_HEADS
    o_z = o_q + MEM_WIDTH
    col = jnp.arange(w_in.shape[1])
    col_scale = jnp.where((col >= o_q) & (col < o_z), MEM_Q_SCALE, 1.0).astype(F32)
    w_bf = (w_in * col_scale).astype(BF16)
    w_main = jnp.concatenate([w_bf[:, :o_dt], w_bf[:, o_q:]], axis=1)
    w_dt = jnp.pad(w_bf[:, o_dt:o_q], ((0, 0), (0, LANES - SSD_HEADS)))
    if h is None:
        (h,) = _norm(x2.reshape(b, t, -1), norm_g)
    proj = _proj(h, w_main, tn=2048, out_dtype=BF16).reshape(b, t, -1)
    dt_raw = _proj(h, w_dt, tn=LANES, out_dtype=F32).reshape(b, t, LANES)
    gated_tok = _ssd_mixer(proj, dt_raw, conv_w, conv_b, dt_bias, a_log, d_skip, ssd_norm_g)
    gated_mem = _mem_attn(proj, kv, q_block=SSD_CONV_DIM // MEM_WIDTH,
                          z_block=(SSD_CONV_DIM + MEM_WIDTH + TOK_WIDTH) // MEM_WIDTH, tq=512)
    return _finish_layer(x2, gated_tok, gated_mem, w_out, next_g, mode)


def _attn_layer(x2, bt, kv, norm_g, w_in, w_out, next_g=None, mode="plain"):
    b, t = bt
    o_q = N_DIL * ATTN_GROUP_COLS
    o_z = o_q + MEM_WIDTH
    x3 = x2.reshape(b, t, -1)
    q_scale = (ATTN_HEAD_DIM ** -0.5) * LOG2E
    col = jnp.arange(w_in.shape[1])
    is_q = (col < o_q) & (col % ATTN_GROUP_COLS < TOK_WIDTH)
    col_scale = jnp.where(is_q, q_scale, jnp.where((col >= o_q) & (col < o_z), MEM_Q_SCALE, 1.0))
    w_all = (w_in * col_scale.astype(F32)).astype(BF16)
    w_rest = jnp.concatenate([w_all[:, o_z:o_z + TOK_WIDTH], w_all[:, o_q:o_z], w_all[:, o_z + TOK_WIDTH:]],
                             axis=1)
    dils = tuple(d for _, d in DILATED_GROUPS)
    hs = dict(zip(dils, _norm(x3, norm_g, dils)))
    rest = _proj(hs[1], w_rest, tn=2560, out_dtype=BF16).reshape(b, t, -1)
    slopes = jnp.exp2(-ALIBI_MAX_EXP * jnp.arange(1, N_DIL * ATTN_HEADS + 1, dtype=F32) / (N_DIL * ATTN_HEADS))
    slopes = slopes.reshape(N_DIL, ATTN_HEADS)
    outs, lses = [], []
    for g, (window, d) in enumerate(DILATED_GROUPS):
        assert window // d == ATTN_BLOCK
        qkv = _proj(hs[d], w_all, tn=2304, out_dtype=BF16, dil=d,
                    col0=g * ATTN_GROUP_COLS, ncols=ATTN_GROUP_COLS)
        o, lse = _dil_attn(qkv, slopes[g] * (float(d) * LOG2E))
        outs.append(o)
        lses.append(lse)
    gated_tok = _attn_mix(outs, lses, rest, tm=256)
    gated_mem = _mem_attn(rest, kv, q_block=TOK_WIDTH // MEM_WIDTH,
                          z_block=(TOK_WIDTH + MEM_WIDTH) // MEM_WIDTH, tq=512)
    return _finish_layer(x2, gated_tok, gated_mem, w_out, next_g, mode)


def kernel(x, mem, mem_norm_g, final_norm_g, norm_g_0, w_in_0, conv_w_0, conv_b_0, dt_bias_0, a_log_0, d_skip_0, ssd_norm_g_0, w_mem_kv_0, w_out_0, norm_g_1, w_in_1, w_mem_kv_1, w_out_1, norm_g_2, w_in_2, conv_w_2, conv_b_2, dt_bias_2, a_log_2, d_skip_2, ssd_norm_g_2, w_mem_kv_2, w_out_2, norm_g_3, w_in_3, w_mem_kv_3, w_out_3):
    b, t, dm = x.shape
    bt = (b, t)
    x2 = x.reshape(b * t, dm)
    (x2,) = _ssd_layer(x2, None, bt, _mem_kv(mem, mem_norm_g, w_mem_kv_0), norm_g_0, w_in_0, conv_w_0, conv_b_0,
                       dt_bias_0, a_log_0, d_skip_0, ssd_norm_g_0, w_out_0)
    x2, h2 = _attn_layer(x2, bt, _mem_kv(mem, mem_norm_g, w_mem_kv_1), norm_g_1, w_in_1, w_out_1,
                         next_g=norm_g_2, mode="norm")
    (x2,) = _ssd_layer(x2, h2.reshape(b, t, dm), bt, _mem_kv(mem, mem_norm_g, w_mem_kv_2), norm_g_2, w_in_2,
                       conv_w_2, conv_b_2, dt_bias_2, a_log_2, d_skip_2, ssd_norm_g_2, w_out_2)
    (y,) = _attn_layer(x2, bt, _mem_kv(mem, mem_norm_g, w_mem_kv_3), norm_g_3, w_in_3, w_out_3,
                       next_g=final_norm_g, mode="final")
    return y.reshape(b, t, dm)
```

```python
import functools

import jax
import jax.numpy as jnp
from jax import lax
from jax.experimental import pallas as pl
from jax.experimental.pallas import tpu as pltpu

F32 = jnp.float32
BF16 = jnp.bfloat16
EPS = 1e-6
NEG = -1e30
LOG2E = 1.4426950408889634

LANES = 128
D_MODEL = 2048
N_MEM = 256
MIX_WIDTH = 2 * D_MODEL
MEM_WIDTH = MIX_WIDTH // 4
TOK_WIDTH = MIX_WIDTH - MEM_WIDTH
MEM_HEADS = 4
MEM_HEAD_DIM = MEM_WIDTH // MEM_HEADS

SSD_HEAD_DIM = 64
SSD_HEADS = TOK_WIDTH // SSD_HEAD_DIM
SSD_GROUPS = 8
SSD_HEADS_PER_GROUP = SSD_HEADS // SSD_GROUPS
SSD_GROUP_COLS = SSD_HEADS_PER_GROUP * SSD_HEAD_DIM
SSD_PAIRS_PER_GROUP = SSD_GROUP_COLS // LANES
SSD_STATE = 128
SSD_CONV = 4
SSD_CHUNK = 128
SSD_BC_COLS = SSD_GROUPS * SSD_STATE
SSD_CONV_DIM = TOK_WIDTH + 2 * SSD_BC_COLS
HALO_ROWS = 8

ATTN_HEAD_DIM = 128
ATTN_HEADS = TOK_WIDTH // ATTN_HEAD_DIM
DILATED_GROUPS = ((128, 1), (512, 4), (2048, 16))
N_DIL = len(DILATED_GROUPS)
ATTN_GROUP_COLS = 3 * TOK_WIDTH
ATTN_BLOCK = 128
ALIBI_MAX_EXP = 8.0

VMEM_LIMIT = 56 * 1024 * 1024


def _cparams(*sem):
    return pltpu.CompilerParams(dimension_semantics=sem, vmem_limit_bytes=VMEM_LIMIT)


def _silu(v):
    hv = 0.5 * v
    return hv + hv * jnp.tanh(hv)


def _split3(v):
    hi = v.astype(BF16)
    r1 = v - hi.astype(F32)
    mid = r1.astype(BF16)
    lo = (r1 - mid.astype(F32)).astype(BF16)
    return hi, mid, lo


ROW_TILE = 1024
PERM_CHUNK = 256
NORM_ROWS = 32


def _norm_kernel(x_ref, g_ref, *out_refs, dils):
    tm = x_ref.shape[0]
    nat_ref = out_refs[0]

    def body(i, carry):
        r = pl.multiple_of(i * NORM_ROWS, NORM_ROWS)
        xf = x_ref[pl.ds(r, NORM_ROWS), :]
        ms = jnp.mean(xf * xf, axis=-1, keepdims=True)
        nat_ref[pl.ds(r, NORM_ROWS), :] = ((xf * lax.rsqrt(ms + EPS)) * g_ref[...]).astype(BF16)
        return carry
    lax.fori_loop(0, tm // NORM_ROWS, body, 0)

    row = lax.broadcasted_iota(jnp.int32, (PERM_CHUNK, PERM_CHUNK), 0)
    col = lax.broadcasted_iota(jnp.int32, (PERM_CHUNK, PERM_CHUNK), 1)
    for d, o_ref in zip(dils[1:], out_refs[1:]):
        sub = tm // d
        csub = PERM_CHUNK // d
        perm = jnp.where((row % csub) * d + row // csub == col, 1.0, 0.0).astype(BF16)
        for c in range(tm // PERM_CHUNK):
            hp = jnp.dot(perm, nat_ref[c * PERM_CHUNK:(c + 1) * PERM_CHUNK, :],
                         preferred_element_type=F32).astype(BF16)
            for r in range(d):
                o_ref[r * sub + c * csub:r * sub + (c + 1) * csub, :] = hp[r * csub:(r + 1) * csub, :]


def _norm(x, g, dils=(1,)):
    b, t, k = x.shape
    tm = min(ROW_TILE, t)
    assert dils[0] == 1 and t % tm == 0 and tm % PERM_CHUNK == 0
    blk = pl.BlockSpec((None, tm, k), lambda bi, i: (bi, i, 0))
    return pl.pallas_call(
        functools.partial(_norm_kernel, dils=dils),
        grid=(b, t // tm),
        in_specs=[blk, pl.BlockSpec((1, k), lambda bi, i: (0, 0))],
        out_specs=[blk] * len(dils),
        out_shape=[jax.ShapeDtypeStruct((b, t, k), BF16)] * len(dils),
        compiler_params=_cparams("parallel", "parallel"),
        name="pre_norm",
    )(x, g.reshape(1, k))


def _proj_kernel(h_ref, w_ref, o_ref):
    res = jnp.dot(h_ref[...], w_ref[...], preferred_element_type=F32).astype(o_ref.dtype)
    o_ref[...] = res.reshape(o_ref.shape)


def _proj(h, w, *, tn, out_dtype, dil=1, col0=0, ncols=None):
    b, t, k = h.shape
    n = w.shape[1] if ncols is None else ncols
    tm = min(ROW_TILE, t)
    tn = min(tn, n)
    assert t % tm == 0 and n % tn == 0 and col0 % tn == 0 and tm % dil == 0
    j0 = col0 // tn
    return pl.pallas_call(
        _proj_kernel,
        grid=(b, t // tm, n // tn),
        in_specs=[pl.BlockSpec((None, tm, k), lambda bi, i, j: (bi, i, 0)),
                  pl.BlockSpec((k, tn), lambda bi, i, j: (0, j0 + j))],
        out_specs=pl.BlockSpec((None, dil, tm // dil, tn), lambda bi, i, j: (bi, 0, i, j)),
        out_shape=jax.ShapeDtypeStruct((b, dil, t // dil, n), out_dtype),
        compiler_params=_cparams("parallel", "parallel", "parallel"),
        name="proj",
    )(h, w)


OUT_ROWS = 512


def _out_proj_kernel(gt_ref, gm_ref, wt_ref, wm_ref, x_ref, *rest, mode):
    g_ref = None if mode == "plain" else rest[0]
    new_ref = rest[-1] if mode != "norm" else rest[-2]
    acc = jnp.dot(gt_ref[...], wt_ref[...], preferred_element_type=F32)
    acc = acc + jnp.dot(gm_ref[...], wm_ref[...], preferred_element_type=F32)
    new_ref[...] = x_ref[...] + acc
    if mode == "plain":
        return
    normed_ref = rest[-1] if mode == "norm" else rest[-2]

    def body(i, carry):
        r = pl.multiple_of(i * NORM_ROWS, NORM_ROWS)
        xf = new_ref[pl.ds(r, NORM_ROWS), :]
        ms = jnp.mean(xf * xf, axis=-1, keepdims=True)
        hn = (xf * lax.rsqrt(ms + EPS)) * g_ref[...]
        normed_ref[pl.ds(r, NORM_ROWS), :] = hn.astype(normed_ref.dtype)
        return carry
    lax.fori_loop(0, new_ref.shape[0] // NORM_ROWS, body, 0, unroll=2)


def _out_proj(gt, gm, wt, wm, x, g=None, *, mode):
    m, n = x.shape
    tm = min(OUT_ROWS, m)
    assert m % tm == 0 and tm % NORM_ROWS == 0 and (g is None) == (mode == "plain")
    rows = lambda width: pl.BlockSpec((tm, width), lambda i: (i, 0))
    resident = lambda arr: pl.BlockSpec(arr.shape, lambda i: (0, 0), pipeline_mode=pl.Buffered(1))
    in_specs = [rows(gt.shape[1]), rows(gm.shape[1]), resident(wt), resident(wm), rows(n)]
    args = [gt, gm, wt, wm, x]
    if g is not None:
        in_specs.append(pl.BlockSpec((1, n), lambda i: (0, 0)))
        args.append(g.reshape(1, n))
    out_shape = [jax.ShapeDtypeStruct((m, n), F32)]
    if mode == "norm":
        out_shape.append(jax.ShapeDtypeStruct((m, n), BF16))
    return pl.pallas_call(
        functools.partial(_out_proj_kernel, mode=mode),
        grid=(m // tm,),
        in_specs=in_specs,
        out_specs=[rows(n)] * len(out_shape),
        out_shape=out_shape,
        scratch_shapes=[pltpu.VMEM((tm, n), F32)] if mode == "final" else [],
        compiler_params=_cparams("parallel"),
        name="out_proj",
    )(*args)


MEM_Q_SCALE = (MEM_HEAD_DIM ** -0.5) * LOG2E


def _mem_attn_kernel(q_ref, z_ref, mk_ref, mv_ref, o_ref):
    for h in range(MEM_HEADS):
        sl = slice(h * MEM_HEAD_DIM, (h + 1) * MEM_HEAD_DIM)
        s = lax.dot_general(q_ref[:, sl], mk_ref[:, sl], (((1,), (1,)), ((), ())),
                            preferred_element_type=F32)
        m = jnp.max(s, axis=-1, keepdims=True)
        p = jnp.exp2(s - m)
        den = jnp.sum(p, axis=-1, keepdims=True)
        o = jnp.dot(p.astype(BF16), mv_ref[:, sl], preferred_element_type=F32) / den
        z = z_ref[:, sl].astype(F32)
        o_ref[:, sl] = (o * _silu(z)).astype(o_ref.dtype)


def _mem_attn(proj, kv, *, q_block, z_block, tq):
    b, t, _ = proj.shape
    tq = min(tq, t)
    return pl.pallas_call(
        _mem_attn_kernel,
        grid=(b, t // tq),
        in_specs=[pl.BlockSpec((None, tq, MEM_WIDTH), lambda bi, i: (bi, i, q_block)),
                  pl.BlockSpec((None, tq, MEM_WIDTH), lambda bi, i: (bi, i, z_block)),
                  pl.BlockSpec((None, N_MEM, MEM_WIDTH), lambda bi, i: (bi, 0, 0)),
                  pl.BlockSpec((None, N_MEM, MEM_WIDTH), lambda bi, i: (bi, 0, 1))],
        out_specs=pl.BlockSpec((None, tq, MEM_WIDTH), lambda bi, i: (bi, i, 0)),
        out_shape=jax.ShapeDtypeStruct((b, t, MEM_WIDTH), BF16),
        compiler_params=_cparams("parallel", "parallel"),
        name="mem_attn",
    )(proj, proj, kv, kv)


def _ssd_kernel(xbc_ref, z_ref, dt_ref, cw_ref, cb_ref, dtb_ref, alog_ref, dskip_ref, ng_ref,
                sel_pair_ref, o_ref, halo_ref, xc_ref, state_ref):
    q = SSD_CHUNK
    c = pl.program_id(1)

    @pl.when(c == 0)
    def _():
        halo_ref[...] = jnp.zeros_like(halo_ref)
        state_ref[...] = jnp.zeros_like(state_ref)

    for t in range(SSD_CONV_DIM // LANES):
        cs = slice(t * LANES, (t + 1) * LANES)
        u = xbc_ref[:, cs].astype(F32)
        ext = jnp.concatenate([halo_ref[:, cs], u], axis=0)
        acc = cb_ref[:, cs] + cw_ref[SSD_CONV - 1:SSD_CONV, cs] * u
        for k in range(SSD_CONV - 1):
            shifted = pltpu.roll(ext, SSD_CONV - 1 - k, axis=0)[HALO_ROWS:, :]
            acc = acc + cw_ref[k:k + 1, cs] * shifted
        xc_ref[:, cs] = _silu(acc)
        halo_ref[:, cs] = u[q - HALO_ROWS:, :]

    dtr = dt_ref[...] + dtb_ref[...]
    dt = jnp.maximum(dtr, 0.0) + jnp.log1p(jnp.exp(-jnp.abs(dtr)))
    a = -jnp.exp(alog_ref[...])
    dta = dt * a
    row = lax.broadcasted_iota(jnp.int32, (q, q), 0)
    col = lax.broadcasted_iota(jnp.int32, (q, q), 1)
    causal = row >= col
    tri = jnp.where(causal, 1.0, 0.0).astype(BF16)
    hi, mid, lo = _split3(dta)
    a_cs = (jnp.dot(tri, hi, preferred_element_type=F32)
            + jnp.dot(tri, mid, preferred_element_type=F32)
            + jnp.dot(tri, lo, preferred_element_type=F32))
    a_cs_t = a_cs.T
    dt_t = dt.T
    a_last = a_cs[q - 1:q, :]
    lane = lax.broadcasted_iota(jnp.int32, (q, LANES), 1)
    first_head = lane < SSD_HEAD_DIM
    eacs_parts = _split3(jnp.exp(a_cs))
    wst_parts = _split3(jnp.exp(a_last - a_cs) * dt)

    def spread(parts, cols):
        return sum(jnp.dot(part, sel_pair_ref[:, cols], preferred_element_type=F32) for part in parts)

    for g in range(SSD_GROUPS):
        pair_cols = slice(g * SSD_GROUP_COLS, (g + 1) * SSD_GROUP_COLS)
        eacs_pairs = spread(eacs_parts, pair_cols)
        wst_pairs = spread(wst_parts, pair_cols)
        b_off = TOK_WIDTH + g * SSD_STATE
        c_off = TOK_WIDTH + SSD_BC_COLS + g * SSD_STATE
        bm = xc_ref[:, b_off:b_off + SSD_STATE]
        cm = xc_ref[:, c_off:c_off + SSD_STATE].astype(BF16)
        bm_t = bm.T.astype(BF16)
        cb = lax.dot_general(cm, bm.astype(BF16), (((1,), (1,)), ((), ())),
                             preferred_element_type=F32)
        sumsq = jnp.zeros((q, 1), F32)
        gated = []
        for pr in range(SSD_PAIRS_PER_GROUP):
            h0 = g * SSD_HEADS_PER_GROUP + 2 * pr
            xs = slice(g * SSD_GROUP_COLS + pr * LANES, g * SSD_GROUP_COLS + (pr + 1) * LANES)
            ss = slice(pr * LANES, (pr + 1) * LANES)
            xp = xc_ref[:, xs]
            xpb = xp.astype(BF16)
            res = []
            for h in (h0, h0 + 1):
                seg = a_cs[:, h:h + 1] - a_cs_t[h:h + 1, :]
                decay = jnp.exp(jnp.where(causal, seg, NEG))
                mat = (cb * decay * dt_t[h:h + 1, :]).astype(BF16)
                res.append(jnp.dot(mat, xpb, preferred_element_type=F32))
            y = jnp.where(first_head, res[0], res[1])
            s_prev = state_ref[g, :, ss]
            e_pair = eacs_pairs[:, ss]
            y = y + jnp.dot(cm, s_prev.astype(BF16), preferred_element_type=F32) * e_pair
            y = y + dskip_ref[:, xs] * xp
            xw = (xp * wst_pairs[:, ss]).astype(BF16)
            chunk_decay = e_pair[q - 1:q, :]
            state_ref[g, :, ss] = chunk_decay * s_prev + jnp.dot(bm_t, xw, preferred_element_type=F32)
            z = z_ref[:, xs].astype(F32)
            gz = y * _silu(z)
            sumsq = sumsq + jnp.sum(gz * gz, axis=-1, keepdims=True)
            gated.append(gz)
        rinv = lax.rsqrt(sumsq * (1.0 / SSD_GROUP_COLS) + EPS)
        for pr in range(SSD_PAIRS_PER_GROUP):
            xs = slice(g * SSD_GROUP_COLS + pr * LANES, g * SSD_GROUP_COLS + (pr + 1) * LANES)
            o_ref[:, xs] = ((gated[pr] * rinv) * ng_ref[:, xs]).astype(o_ref.dtype)


def _ssd_mixer(proj, dt_raw, conv_w, conv_b, dt_bias, a_log, d_skip, ssd_norm_g):
    b, t, _ = proj.shape
    q = SSD_CHUNK
    pad = LANES - SSD_HEADS
    full = lambda bi, ci: (0, 0)
    head_of_row = jnp.arange(LANES)[:, None]
    return pl.pallas_call(
        _ssd_kernel,
        grid=(b, t // q),
        in_specs=[pl.BlockSpec((None, q, SSD_CONV_DIM), lambda bi, ci: (bi, ci, 0)),
                  pl.BlockSpec((None, q, TOK_WIDTH), lambda bi, ci: (bi, ci, 2)),
                  pl.BlockSpec((None, q, LANES), lambda bi, ci: (bi, ci, 0)),
                  pl.BlockSpec((SSD_CONV, SSD_CONV_DIM), full),
                  pl.BlockSpec((1, SSD_CONV_DIM), full),
                  pl.BlockSpec((1, LANES), full),
                  pl.BlockSpec((1, LANES), full),
                  pl.BlockSpec((1, TOK_WIDTH), full),
                  pl.BlockSpec((1, TOK_WIDTH), full),
                  pl.BlockSpec((LANES, TOK_WIDTH), full)],
        out_specs=pl.BlockSpec((None, q, TOK_WIDTH), lambda bi, ci: (bi, ci, 0)),
        out_shape=jax.ShapeDtypeStruct((b, t, TOK_WIDTH), BF16),
        scratch_shapes=[pltpu.VMEM((HALO_ROWS, SSD_CONV_DIM), F32),
                        pltpu.VMEM((q, SSD_CONV_DIM), F32),
                        pltpu.VMEM((SSD_GROUPS, SSD_STATE, SSD_GROUP_COLS), F32)],
        compiler_params=_cparams("parallel", "arbitrary"),
        name="ssd_mixer",
    )(proj, proj, dt_raw, conv_w, conv_b.reshape(1, -1),
      jnp.pad(dt_bias, (0, pad)).reshape(1, LANES), jnp.pad(a_log, (0, pad)).reshape(1, LANES),
      jnp.repeat(d_skip, SSD_HEAD_DIM).reshape(1, TOK_WIDTH), ssd_norm_g.reshape(1, TOK_WIDTH),
      (head_of_row == jnp.arange(TOK_WIDTH)[None, :] // SSD_HEAD_DIM).astype(BF16))


ATTN_SLOTS = 4


def _dil_attn_kernel(slope_ref, q_ref, k_ref, v_ref, o_ref, lse_ref,
                     bias_ref, kt_ref, ktp_ref, vp_ref, s_ref, p_ref):
    c = ATTN_BLOCK
    n = pl.program_id(2)

    @pl.when((pl.program_id(0) == 0) & (pl.program_id(1) == 0) & (n == 0))
    def _():
        row = lax.broadcasted_iota(jnp.int32, (c, 2 * c), 0)
        col = lax.broadcasted_iota(jnp.int32, (c, 2 * c), 1)
        rel_pc = row + c - col
        rel_cp = jnp.where(col < c, row - col, row + 2 * c - col)
        band = lambda rel: (rel >= 0) & (rel <= c)
        for h in range(ATTN_HEADS):
            slope = slope_ref[h]
            bias_ref[0, h] = jnp.where(band(rel_pc), -slope * rel_pc.astype(F32), NEG)
            bias_ref[1, h] = jnp.where(band(rel_cp), -slope * rel_cp.astype(F32), NEG)
            bias_ref[2, h] = jnp.where(band(rel_cp) & (col < c), -slope * rel_cp.astype(F32), NEG)
        ktp_ref[...] = jnp.zeros_like(ktp_ref)
        vp_ref[...] = jnp.zeros_like(vp_ref)

    table_a = jnp.where(n == 0, 2, 1)
    lane = lax.broadcasted_iota(jnp.int32, (c, LANES), 1)
    lse_ref[...] = jnp.zeros_like(lse_ref)

    def head_cols(h):
        return slice(h * ATTN_HEAD_DIM, (h + 1) * ATTN_HEAD_DIM)

    def transpose_keys(blk, h):
        kt_ref[blk, head_cols(h), :] = k_ref[blk * c:(blk + 1) * c, head_cols(h)].T

    def scores(blk, h, slot):
        kt_a = kt_ref[0, head_cols(h), :]
        if blk == 0:
            kt_w = jnp.concatenate([kt_a, ktp_ref[head_cols(h), :]], axis=1)
            bias = bias_ref[table_a, h]
        else:
            kt_w = jnp.concatenate([kt_a, kt_ref[1, head_cols(h), :]], axis=1)
            bias = bias_ref[0, h]
        s = jnp.dot(q_ref[blk * c:(blk + 1) * c, head_cols(h)], kt_w, preferred_element_type=F32)
        s_ref[slot] = s + bias

    def softmax(blk, h, slot):
        s = s_ref[slot]
        m = jnp.max(s, axis=-1, keepdims=True)
        p = jnp.exp2(s - m)
        den = jnp.sum(p, axis=-1, keepdims=True)
        p_ref[slot] = p.astype(BF16)
        rows = slice(blk * c, (blk + 1) * c)
        lse_ref[rows, :] = jnp.where(lane == h, m, jnp.where(lane == ATTN_HEADS + h, den, lse_ref[rows, :]))

    def values(blk, h, slot):
        if blk == 0:
            vw = jnp.concatenate([v_ref[0:c, head_cols(h)], vp_ref[:, head_cols(h)]], axis=0)
        else:
            vw = v_ref[:, head_cols(h)]
        o_ref[blk * c:(blk + 1) * c, head_cols(h)] = jnp.dot(
            p_ref[slot], vw, preferred_element_type=F32).astype(o_ref.dtype)

    for blk in range(2):
        for h in range(ATTN_HEADS):
            transpose_keys(blk, h)

    items = [(blk, h) for blk in range(2) for h in range(ATTN_HEADS)]
    for i in range(len(items) + 2):
        if i < len(items):
            scores(*items[i], i % ATTN_SLOTS)
        if 1 <= i <= len(items):
            softmax(*items[i - 1], (i - 1) % ATTN_SLOTS)
        if i >= 2:
            values(*items[i - 2], (i - 2) % ATTN_SLOTS)

    ktp_ref[...] = kt_ref[1]
    vp_ref[...] = v_ref[c:2 * c, :]


def _dil_attn(qkv, slopes_d):
    b, d, n_sub, _ = qkv.shape
    c = ATTN_BLOCK
    assert n_sub % (2 * c) == 0
    blk = (None, None, 2 * c, TOK_WIDTH)
    col_block = lambda off: (lambda bi, r, n, *_: (bi, r, n, off))
    return pl.pallas_call(
        _dil_attn_kernel,
        grid_spec=pltpu.PrefetchScalarGridSpec(
            num_scalar_prefetch=1,
            grid=(b, d, n_sub // (2 * c)),
            in_specs=[pl.BlockSpec(blk, col_block(0)), pl.BlockSpec(blk, col_block(1)),
                      pl.BlockSpec(blk, col_block(2))],
            out_specs=[pl.BlockSpec(blk, col_block(0)),
                       pl.BlockSpec((None, None, 2 * c, LANES), col_block(0))],
            scratch_shapes=[pltpu.VMEM((3, ATTN_HEADS, c, 2 * c), F32),
                            pltpu.VMEM((2, TOK_WIDTH, c), BF16),
                            pltpu.VMEM((TOK_WIDTH, c), BF16),
                            pltpu.VMEM((c, TOK_WIDTH), BF16),
                            pltpu.VMEM((ATTN_SLOTS, c, 2 * c), F32),
                            pltpu.VMEM((ATTN_SLOTS, c, 2 * c), BF16)]),
        out_shape=[jax.ShapeDtypeStruct((b, d, n_sub, TOK_WIDTH), BF16),
                   jax.ShapeDtypeStruct((b, d, n_sub, LANES), F32)],
        compiler_params=_cparams("arbitrary", "arbitrary", "arbitrary"),
        name="dil_attn",
    )(slopes_d, qkv, qkv, qkv)


def _attn_mix_kernel(o0_ref, o1_ref, o2_ref, l0_ref, l1_ref, l2_ref, z_ref, out_ref, *, dils):
    tm = out_ref.shape[0]
    row = lax.broadcasted_iota(jnp.int32, (tm, tm), 0)
    col = lax.broadcasted_iota(jnp.int32, (tm, tm), 1)

    def perm(d):
        sub = tm // d
        return jnp.where((row % d) * sub + row // d == col, 1.0, 0.0).astype(BF16)

    def stacked(ref, d):
        return ref[0] if d == 1 else jnp.concatenate([ref[r] for r in range(d)], axis=0)

    perms = [None if d == 1 else perm(d) for d in dils]
    o_refs = (o0_ref, o1_ref, o2_ref)
    maxes, dens = [], []
    for ref, d, pm in zip((l0_ref, l1_ref, l2_ref), dils, perms):
        st = stacked(ref, d)
        if pm is not None:
            st = sum(jnp.dot(pm, part, preferred_element_type=F32) for part in _split3(st))
        maxes.append(st)
        dens.append(pltpu.roll(st, LANES - ATTN_HEADS, axis=1))
    mx = jnp.maximum(jnp.maximum(maxes[0], maxes[1]), maxes[2])
    es = [jnp.exp2(m - mx) for m in maxes]
    inv = 1.0 / (es[0] * dens[0] + es[1] * dens[1] + es[2] * dens[2])
    ws = [e * inv for e in es]
    for h in range(ATTN_HEADS):
        sl = slice(h * ATTN_HEAD_DIM, (h + 1) * ATTN_HEAD_DIM)
        y = None
        for ref, d, pm, w in zip(o_refs, dils, perms, ws):
            if d == 1:
                o = ref[0, :, sl].astype(F32)
            else:
                o = jnp.dot(pm, jnp.concatenate([ref[r, :, sl] for r in range(d)], axis=0),
                            preferred_element_type=F32)
            term = jnp.broadcast_to(w[:, h:h + 1], (tm, ATTN_HEAD_DIM)) * o
            y = term if y is None else y + term
        out_ref[:, sl] = (y * _silu(z_ref[:, sl].astype(F32))).astype(out_ref.dtype)


def _attn_mix(os_, ls_, rest, *, tm):
    b, t, _ = rest.shape
    tm = min(tm, t)
    dils = tuple(o.shape[1] for o in os_)
    grp = lambda d, w: pl.BlockSpec((None, d, tm // d, w), lambda bi, i: (bi, 0, i, 0))
    tok = pl.BlockSpec((None, tm, TOK_WIDTH), lambda bi, i: (bi, i, 0))
    return pl.pallas_call(
        functools.partial(_attn_mix_kernel, dils=dils),
        grid=(b, t // tm),
        in_specs=[grp(d, TOK_WIDTH) for d in dils] + [grp(d, LANES) for d in dils] + [tok],
        out_specs=tok,
        out_shape=jax.ShapeDtypeStruct((b, t, TOK_WIDTH), BF16),
        compiler_params=_cparams("parallel", "parallel"),
        name="attn_mix",
    )(*os_, *ls_, rest)


def _mem_kv(mem, mem_norm_g, w_mem_kv):
    b, n, dm = mem.shape
    (hm,) = _norm(mem.reshape(1, b * n, dm), mem_norm_g)
    kv = _proj(hm, w_mem_kv.astype(BF16), tn=1024, out_dtype=BF16)
    return kv.reshape(b, n, 2 * MEM_WIDTH)


def _finish_layer(x2, gated_tok, gated_mem, w_out, next_g, mode):
    m = x2.shape[0]
    wt = w_out[:TOK_WIDTH].astype(BF16)
    wm = w_out[TOK_WIDTH:].astype(BF16)
    return _out_proj(gated_tok.reshape(m, TOK_WIDTH), gated_mem.reshape(m, MEM_WIDTH), wt, wm, x2,
                     next_g, mode=mode)


def _ssd_layer(x2, h, bt, kv, norm_g, w_in, conv_w, conv_b, dt_bias, a_log, d_skip, ssd_norm_g, w_out,
               next_g=None, mode="plain"):
    b, t = bt
    o_dt = SSD_CONV_DIM
    o_q = o_dt + SSD_HEADS
    o_z = o_q + MEM_WIDTH
    col = jnp.arange(w_in.shape[1])
    col_scale = jnp.where((col >= o_q) & (col < o_z), MEM_Q_SCALE, 1.0).astype(F32)
    w_bf = (w_in * col_scale).astype(BF16)
    w_main = jnp.concatenate([w_bf[:, :o_dt], w_bf[:, o_q:]], axis=1)
    w_dt = jnp.pad(w_bf[:, o_dt:o_q], ((0, 0), (0, LANES - SSD_HEADS)))
    if h is None:
        (h,) = _norm(x2.reshape(b, t, -1), norm_g)
    proj = _proj(h, w_main, tn=2048, out_dtype=BF16).reshape(b, t, -1)
    dt_raw = _proj(h, w_dt, tn=LANES, out_dtype=F32).reshape(b, t, LANES)
    gated_tok = _ssd_mixer(proj, dt_raw, conv_w, conv_b, dt_bias, a_log, d_skip, ssd_norm_g)
    gated_mem = _mem_attn(proj, kv, q_block=SSD_CONV_DIM // MEM_WIDTH,
                          z_block=(SSD_CONV_DIM + MEM_WIDTH + TOK_WIDTH) // MEM_WIDTH, tq=1024)
    return _finish_layer(x2, gated_tok, gated_mem, w_out, next_g, mode)


def _attn_layer(x2, bt, kv, norm_g, w_in, w_out, next_g=None, mode="plain"):
    b, t = bt
    o_q = N_DIL * ATTN_GROUP_COLS
    o_z = o_q + MEM_WIDTH
    x3 = x2.reshape(b, t, -1)
    q_scale = (ATTN_HEAD_DIM ** -0.5) * LOG2E
    col = jnp.arange(w_in.shape[1])
    is_q = (col < o_q) & (col % ATTN_GROUP_COLS < TOK_WIDTH)
    col_scale = jnp.where(is_q, q_scale, jnp.where((col >= o_q) & (col < o_z), MEM_Q_SCALE, 1.0))
    w_all = (w_in * col_scale.astype(F32)).astype(BF16)
    w_rest = jnp.concatenate([w_all[:, o_z:o_z + TOK_WIDTH], w_all[:, o_q:o_z], w_all[:, o_z + TOK_WIDTH:]],
                             axis=1)
    dils = tuple(d for _, d in DILATED_GROUPS)
    hs = dict(zip(dils, _norm(x3, norm_g, dils)))
    rest = _proj(hs[1], w_rest, tn=2560, out_dtype=BF16).reshape(b, t, -1)
    slopes = jnp.exp2(-ALIBI_MAX_EXP * jnp.arange(1, N_DIL * ATTN_HEADS + 1, dtype=F32) / (N_DIL * ATTN_HEADS))
    slopes = slopes.reshape(N_DIL, ATTN_HEADS)
    outs, lses = [], []
    for g, (window, d) in enumerate(DILATED_GROUPS):
        assert window // d == ATTN_BLOCK
        qkv = _proj(hs[d], w_all, tn=3072, out_dtype=BF16, dil=d,
                    col0=g * ATTN_GROUP_COLS, ncols=ATTN_GROUP_COLS)
        o, lse = _dil_attn(qkv, slopes[g] * (float(d) * LOG2E))
        outs.append(o)
        lses.append(lse)
    gated_tok = _attn_mix(outs, lses, rest, tm=256)
    gated_mem = _mem_attn(rest, kv, q_block=TOK_WIDTH // MEM_WIDTH,
                          z_block=(TOK_WIDTH + MEM_WIDTH) // MEM_WIDTH, tq=1024)
    return _finish_layer(x2, gated_tok, gated_mem, w_out, next_g, mode)


def kernel(x, mem, mem_norm_g, final_norm_g, norm_g_0, w_in_0, conv_w_0, conv_b_0, dt_bias_0, a_log_0, d_skip_0, ssd_norm_g_0, w_mem_kv_0, w_out_0, norm_g_1, w_in_1, w_mem_kv_1, w_out_1, norm_g_2, w_in_2, conv_w_2, conv_b_2, dt_bias_2, a_log_2, d_skip_2, ssd_norm_g_2, w_mem_kv_2, w_out_2, norm_g_3, w_in_3, w_mem_kv_3, w_out_3):
    b, t, dm = x.shape
    bt = (b, t)
    x2 = x.reshape(b * t, dm)
    (x2,) = _ssd_layer(x2, None, bt, _mem_kv(mem, mem_norm_g, w_mem_kv_0), norm_g_0, w_in_0, conv_w_0, conv_b_0,
                       dt_bias_0, a_log_0, d_skip_0, ssd_norm_g_0, w_out_0)
    x2, h2 = _attn_layer(x2, bt, _mem_kv(mem, mem_norm_g, w_mem_kv_1), norm_g_1, w_in_1, w_out_1,
                         next_g=norm_g_2, mode="norm")
    (x2,) = _ssd_layer(x2, h2.reshape(b, t, dm), bt, _mem_kv(mem, mem_norm_g, w_mem_kv_2), norm_g_2, w_in_2,
                       conv_w_2, conv_b_2, dt_bias_2, a_log_2, d_skip_2, ssd_norm_g_2, w_out_2)
    (y,) = _attn_layer(x2, bt, _mem_kv(mem, mem_norm_g, w_mem_kv_3), norm_g_3, w_in_3, w_out_3,
                       next_g=final_norm_g, mode="final")
    return y.reshape(b, t, dm)
```
